```python
import math
import jax
import jax.numpy as jnp
from jax import lax
import numpy as np

D_MODEL = 1024
BATCH = 8
SEQ = 2048
DEPTH = 4

HEAD_DIM = 64
DIFF_W = (D_MODEL * 3 // 8)
DELTA_W = (D_MODEL * 3 // 8)
CONV_CH = D_MODEL - DIFF_W - DELTA_W
DIFF_HEADS = DIFF_W // HEAD_DIM
DIFF_DIM = HEAD_DIM // 2
DELTA_HEADS = DELTA_W // HEAD_DIM
CONV_WIDTH = 31
SHORT_CONV = 3
CHUNK = 64
Q_BLOCK = 128
D_FF = 4 * D_MODEL
IN_W = 2 * CONV_CH + 3 * DIFF_W + 4 * DELTA_W + 4 * DELTA_HEADS
NORM_EPS = 1e-6

kernel_name = "hybrid_conv_diffattn_gdn_encoder"


def rms_norm(x, w, eps=NORM_EPS):
    xf = x.astype(jnp.float32)
    y = xf * lax.rsqrt(jnp.mean(xf * xf, axis=-1, keepdims=True) + eps)
    return (y * w.astype(jnp.float32)).astype(x.dtype)


def layer_norm(x, w, b, eps=1e-5):
    xf = x.astype(jnp.float32)
    mu = jnp.mean(xf, axis=-1, keepdims=True)
    xc = xf - mu
    var = jnp.mean(xc * xc, axis=-1, keepdims=True)
    y = xc * lax.rsqrt(var + eps) * w.astype(jnp.float32) + b.astype(jnp.float32)
    return y.astype(x.dtype)


def l2norm(x, eps=1e-6):
    xf = x.astype(jnp.float32)
    return xf * lax.rsqrt(jnp.sum(xf * xf, axis=-1, keepdims=True) + eps)


def depthwise_conv(x, w):
    k, c = w.shape
    pad = (k - 1) // 2
    return lax.conv_general_dilated(
        x, w[:, None, :].astype(x.dtype), window_strides=(1,), padding=[(pad, pad)],
        dimension_numbers=("NWC", "WIO", "NWC"), feature_group_count=c)


def alibi_slopes(n):
    def pow2(m):
        start = 2.0 ** (-8.0 / m)
        return [start ** (i + 1) for i in range(m)]
    if math.log2(n).is_integer():
        s = pow2(n)
    else:
        c = 2 ** int(math.floor(math.log2(n)))
        s = pow2(c) + pow2(2 * c)[0::2][: n - c]
    return jnp.asarray(s, dtype=jnp.float32)


def conv_module(u, dw_w, dw_b, ln_w, ln_b):
    a, g = jnp.split(u, 2, axis=-1)
    h = a * jax.nn.sigmoid(g)
    h = depthwise_conv(h, dw_w) + dw_b.astype(h.dtype)
    h = layer_norm(h, ln_w, ln_b)
    return jax.nn.silu(h)


def diff_attention(q, k, v, lam_q1, lam_k1, lam_q2, lam_k2, subln_w, layer_idx):
    B, S = q.shape[0], q.shape[1]
    H = DIFF_HEADS
    f32 = jnp.float32
    lambda_init = 0.8 - 0.6 * math.exp(-0.3 * layer_idx)
    lam = (jnp.exp(jnp.sum(lam_q1.astype(f32) * lam_k1.astype(f32)))
           - jnp.exp(jnp.sum(lam_q2.astype(f32) * lam_k2.astype(f32))) + lambda_init)
    slopes = jnp.repeat(alibi_slopes(H), 2)
    n_blk = S // Q_BLOCK
    qf = q.astype(f32).reshape(B, n_blk, Q_BLOCK, 2 * H, DIFF_DIM) * (DIFF_DIM ** -0.5)
    qf = jnp.moveaxis(qf, 1, 0)
    kf = k.astype(f32).reshape(B, S, 2 * H, DIFF_DIM)
    vf = v.astype(f32).reshape(B, S, H, HEAD_DIM)
    kpos = jnp.arange(S)

    def block(args):
        q_blk, blk = args
        s = jnp.einsum("bqmd,bkmd->bmqk", q_blk, kf)
        qpos = blk * Q_BLOCK + jnp.arange(Q_BLOCK)
        dist = jnp.abs(qpos[:, None] - kpos[None, :]).astype(f32)
        p = jax.nn.softmax(s - slopes[:, None, None] * dist, axis=-1)
        p = p.reshape(B, H, 2, Q_BLOCK, S)
        a = p[:, :, 0] - lam * p[:, :, 1]
        return jnp.einsum("bhqk,bkhe->bqhe", a, vf)

    o = lax.map(block, (qf, jnp.arange(n_blk)))
    o = jnp.moveaxis(o, 0, 1).reshape(B, S, H, HEAD_DIM)
    o = rms_norm(o, subln_w, eps=1e-5) * (1.0 - lambda_init)
    return o.reshape(B, S, H * HEAD_DIM).astype(q.dtype)


def chunk_gated_delta(q, k, v, g, beta):
    B, S, H, Dk = q.shape
    Dv = v.shape[-1]
    N = S // CHUNK
    q = q * (Dk ** -0.5)

    def chunks(t):
        return t.reshape(B, N, CHUNK, H, -1).transpose(0, 3, 1, 2, 4)

    qc, kc, vc = chunks(q), chunks(k), chunks(v)
    gc = jnp.cumsum(g.reshape(B, N, CHUNK, H).transpose(0, 3, 1, 2), axis=-1)
    bc = beta.reshape(B, N, CHUNK, H).transpose(0, 3, 1, 2)
    idx = jnp.arange(CHUNK)
    lower = idx[:, None] >= idx[None, :]
    strict = idx[:, None] > idx[None, :]
    decay = jnp.exp(jnp.where(lower, gc[..., :, None] - gc[..., None, :], -jnp.inf))
    kb = kc * bc[..., None]
    A = jnp.where(strict, jnp.einsum("bhncd,bhnsd->bhncs", kb, kc) * decay, 0.0)
    rhs = jnp.concatenate([vc * bc[..., None], kb * jnp.exp(gc)[..., None]], axis=-1)
    sol = lax.linalg.triangular_solve(A, rhs, left_side=True, lower=True, unit_diagonal=True)
    u, w = sol[..., :Dv], sol[..., Dv:]
    Aqk = jnp.einsum("bhncd,bhnsd->bhncs", qc, kc) * decay
    qg = qc * jnp.exp(gc)[..., None]
    kdec = kc * jnp.exp(gc[..., -1:] - gc)[..., None]
    glast = jnp.exp(gc[..., -1])

    def step(state, inp):
        u_i, w_i, qg_i, aqk_i, kdec_i, gl_i = inp
        v_new = u_i - jnp.einsum("bhcd,bhde->bhce", w_i, state)
        o = jnp.einsum("bhcd,bhde->bhce", qg_i, state) + jnp.einsum("bhcs,bhse->bhce", aqk_i, v_new)
        state = state * gl_i[..., None, None] + jnp.einsum("bhcd,bhce->bhde", kdec_i, v_new)
        return state, o

    xs = tuple(jnp.moveaxis(t, 2, 0) for t in (u, w, qg, Aqk, kdec, glast))
    s0 = jnp.zeros((B, H, Dk, Dv), jnp.float32)
    _, o = lax.scan(step, s0, xs)
    return o.transpose(1, 0, 3, 2, 4).reshape(B, S, H, Dv)


def gated_deltanet(qkv, z, b_f, b_b, a_f, a_b, conv_w, A_log, dt_bias, norm_w):
    B, S, _ = qkv.shape
    H, D = DELTA_HEADS, HEAD_DIM
    f32 = jnp.float32
    out_dtype = qkv.dtype
    qkv = jax.nn.silu(depthwise_conv(qkv, conv_w))
    q, k, v = jnp.split(qkv, 3, axis=-1)
    q = l2norm(q.reshape(B, S, H, D))
    k = l2norm(k.reshape(B, S, H, D))
    v = v.reshape(B, S, H, D).astype(f32)

    def log_decay(a, a_log, dtb):
        return -jnp.exp(a_log.astype(f32)) * jax.nn.softplus(a.astype(f32) + dtb.astype(f32))

    g_f = log_decay(a_f, A_log[0], dt_bias[0])
    g_b = log_decay(a_b, A_log[1], dt_bias[1])
    beta_f = jax.nn.sigmoid(b_f.astype(f32))
    beta_b = jax.nn.sigmoid(b_b.astype(f32))
    o_f = chunk_gated_delta(q, k, v, g_f, beta_f)
    rev = lambda t: jnp.flip(t, axis=1)
    o_b = rev(chunk_gated_delta(rev(q), rev(k), rev(v), rev(g_b), rev(beta_b)))
    o = rms_norm(o_f + o_b, norm_w) * jax.nn.silu(z.astype(f32).reshape(B, S, H, D))
    return o.reshape(B, S, H * D).astype(out_dtype)


def setup_inputs(seed: int = 0) -> dict:
    key = jax.random.key(seed)
    ks = jax.random.split(key, 24)
    f32 = jnp.float32
    nrm = lambda k, shape, scale: jax.random.normal(k, shape, f32) * scale
    gain = lambda k, shape: 1.0 + 0.05 * jax.random.normal(k, shape, f32)
    dt = jnp.exp(jax.random.uniform(ks[20], (DEPTH, 2, DELTA_HEADS), f32, math.log(1e-3), math.log(1e-1)))
    return {
        "x": nrm(ks[0], (BATCH, SEQ, D_MODEL), 1.0),
        "w_in": nrm(ks[1], (DEPTH, D_MODEL, IN_W), D_MODEL ** -0.5),
        "w_out": nrm(ks[2], (DEPTH, D_MODEL, D_MODEL), D_MODEL ** -0.5),
        "pre_mix_w": gain(ks[3], (DEPTH, D_MODEL)),
        "post_mix_w": gain(ks[4], (DEPTH, D_MODEL)),
        "pre_mlp_w": gain(ks[5], (DEPTH, D_MODEL)),
        "post_mlp_w": gain(ks[6], (DEPTH, D_MODEL)),
        "w_ff1": nrm(ks[7], (DEPTH, D_MODEL, D_FF), D_MODEL ** -0.5),
        "w_ff2": nrm(ks[8], (DEPTH, D_FF, D_MODEL), D_FF ** -0.5),
        "conv_dw_w": nrm(ks[9], (DEPTH, CONV_WIDTH, CONV_CH), CONV_WIDTH ** -0.5),
        "conv_dw_b": nrm(ks[10], (DEPTH, CONV_CH), 0.02),
        "conv_ln_w": gain(ks[11], (DEPTH, CONV_CH)),
        "conv_ln_b": nrm(ks[12], (DEPTH, CONV_CH), 0.02),
        "diff_lambda_q1": nrm(ks[13], (DEPTH, DIFF_DIM), 0.1),
        "diff_lambda_k1": nrm(ks[14], (DEPTH, DIFF_DIM), 0.1),
        "diff_lambda_q2": nrm(ks[15], (DEPTH, DIFF_DIM), 0.1),
        "diff_lambda_k2": nrm(ks[16], (DEPTH, DIFF_DIM), 0.1),
        "diff_subln_w": gain(ks[17], (DEPTH, HEAD_DIM)),
        "delta_conv_w": nrm(ks[18], (DEPTH, SHORT_CONV, 3 * DELTA_W), SHORT_CONV ** -0.5),
        "delta_A_log": jnp.log(jax.random.uniform(ks[19], (DEPTH, 2, DELTA_HEADS), f32, 1.0, 16.0)),
        "delta_dt_bias": dt + jnp.log(-jnp.expm1(-dt)),
        "delta_norm_w": gain(ks[21], (DEPTH, HEAD_DIM)),
    }


def reference(x, w_in, w_out, pre_mix_w, post_mix_w, pre_mlp_w, post_mlp_w, w_ff1, w_ff2,
              conv_dw_w, conv_dw_b, conv_ln_w, conv_ln_b,
              diff_lambda_q1, diff_lambda_k1, diff_lambda_q2, diff_lambda_k2, diff_subln_w,
              delta_conv_w, delta_A_log, delta_dt_bias, delta_norm_w):
    sizes = [2 * CONV_CH, DIFF_W, DIFF_W, DIFF_W, 3 * DELTA_W, DELTA_W,
             DELTA_HEADS, DELTA_HEADS, DELTA_HEADS, DELTA_HEADS]
    offs = [sum(sizes[: i + 1]) for i in range(len(sizes) - 1)]
    for l in range(DEPTH):
        h = rms_norm(x, pre_mix_w[l])
        u = h @ w_in[l]
        (u_conv, dq, dk, dv, gqkv, gz, b_f, b_b, a_f, a_b) = jnp.split(u, offs, axis=-1)
        y_conv = conv_module(u_conv, conv_dw_w[l], conv_dw_b[l], conv_ln_w[l], conv_ln_b[l])
        y_diff = diff_attention(dq, dk, dv, diff_lambda_q1[l], diff_lambda_k1[l],
                                diff_lambda_q2[l], diff_lambda_k2[l], diff_subln_w[l], l)
        y_delta = gated_deltanet(gqkv, gz, b_f, b_b, a_f, a_b, delta_conv_w[l],
                                 delta_A_log[l], delta_dt_bias[l], delta_norm_w[l])
        y = jnp.concatenate([y_conv, y_diff, y_delta], axis=-1) @ w_out[l]
        x = x + rms_norm(y, post_mix_w[l])
        h = rms_norm(x, pre_mlp_w[l])
        y = jnp.square(jax.nn.relu(h @ w_ff1[l])) @ w_ff2[l]
        x = x + rms_norm(y, post_mlp_w[l])
    return x
```

```python
import functools
import math

import jax
import jax.numpy as jnp
from jax import lax
from jax.experimental import pallas as pl
from jax.experimental.pallas import tpu as pltpu

F32 = jnp.float32
BF16 = jnp.bfloat16

HEAD_DIM = 64
DIFF_DIM = 32
CONV_WIDTH = 31
SHORT_CONV = 3
CHUNK = 64
NORM_EPS = 1e-6
LANES = 128
VMEM_LIMIT = 56 * 1024 * 1024


def _alibi_slopes(n):
    def pow2(m):
        start = 2.0 ** (-8.0 / m)
        return [start ** (i + 1) for i in range(m)]
    if math.log2(n).is_integer():
        return pow2(n)
    c = 2 ** int(math.floor(math.log2(n)))
    return pow2(c) + pow2(2 * c)[0::2][: n - c]


def _rms(x, w, eps):
    return x * lax.rsqrt(jnp.mean(x * x, axis=-1, keepdims=True) + eps) * w


def _mm(a, b):
    return jnp.dot(a, b, preferred_element_type=F32)


def _mm_nt(a, b):
    return lax.dot_general(a, b, (((1,), (1,)), ((), ())), preferred_element_type=F32)


def _mm_tn(a, b):
    return lax.dot_general(a, b, (((0,), (0,)), ((), ())), preferred_element_type=F32)


def _mm_exact_rhs(x, w):
    hi = x.astype(BF16)
    r1 = x - hi.astype(F32)
    mid = r1.astype(BF16)
    lo = (r1 - mid.astype(F32)).astype(BF16)
    return _mm(hi, w) + _mm(mid, w) + _mm(lo, w)


def _softplus(x):
    return jnp.maximum(x, 0.0) + jnp.log1p(jnp.exp(-jnp.abs(x)))


def _silu(x):
    return x * jax.nn.sigmoid(x)


def _in_proj_kernel(x_ref, nw_ref, w_ref, uconv_ref, dqkv_ref, gqkv_ref, gz_ref, gate_ref, *, splits):
    hb = _rms(x_ref[...], nw_ref[...], NORM_EPS).astype(BF16)
    outs = (uconv_ref, dqkv_ref, gqkv_ref, gz_ref, gate_ref)
    off = 0
    for o_ref, width in zip(outs, splits):
        o_ref[...] = _mm(hb, w_ref[:, off:off + width]).astype(o_ref.dtype)
        off += width


def _in_proj(x2, nw, w_all, layer, splits, tm):
    m, d = x2.shape
    npad = w_all.shape[-1]
    dts = (BF16, BF16, BF16, BF16, F32)
    return pl.pallas_call(
        functools.partial(_in_proj_kernel, splits=splits),
        grid=(m // tm,),
        in_specs=[
            pl.BlockSpec((tm, d), lambda i: (i, 0)),
            pl.BlockSpec((None, 1, d), lambda i: (layer, 0, 0)),
            pl.BlockSpec((None, d, npad), lambda i: (layer, 0, 0)),
        ],
        out_specs=[pl.BlockSpec((tm, w), lambda i: (i, 0)) for w in splits],
        out_shape=[jax.ShapeDtypeStruct((m, w), dt) for w, dt in zip(splits, dts)],
        compiler_params=pltpu.CompilerParams(dimension_semantics=("arbitrary",), vmem_limit_bytes=VMEM_LIMIT),
        name="in_proj",
    )(x2, nw, w_all)


def _convmod_kernel(u_ref, w_ref, b_ref, lnw_ref, lnb_ref, o_ref, pad_ref, *, seq, ch, tile):
    front = 16
    u = u_ref[...].astype(F32)
    h = u[:, :ch] * jax.nn.sigmoid(u[:, ch:])
    pad_ref[0:front, :] = jnp.zeros((front, ch), F32)
    pad_ref[front:front + seq, :] = h
    tail = pad_ref.shape[0] - front - seq
    pad_ref[front + seq:, :] = jnp.zeros((tail, ch), F32)
    win_rows = tile + 40
    half = (CONV_WIDTH - 1) // 2

    def body(t, carry):
        base = pl.multiple_of(t * tile, tile)
        win = pad_ref[pl.ds(base, win_rows), :]
        acc = jnp.zeros((tile, ch), F32)
        for r in range(8):
            wr = win[r:r + tile + 32]
            for a in range(4):
                k = 8 * a + r - (front - half)
                if 0 <= k < CONV_WIDTH:
                    acc = acc + wr[8 * a:8 * a + tile] * w_ref[k:k + 1, :]
        acc = acc + b_ref[...]
        mu = jnp.mean(acc, axis=-1, keepdims=True)
        xc = acc - mu
        var = jnp.mean(xc * xc, axis=-1, keepdims=True)
        y = xc * lax.rsqrt(var + 1e-5) * lnw_ref[...] + lnb_ref[...]
        o_ref[pl.ds(base, tile), :] = _silu(y).astype(o_ref.dtype)
        return carry

    lax.fori_loop(0, seq // tile, body, 0)


def _convmod(uconv, dw_w, dw_b, ln_w, ln_b, layer):
    b, s, c2 = uconv.shape
    ch = c2 // 2
    tile = 64
    vec = lambda: pl.BlockSpec((None, 1, ch), lambda i: (layer, 0, 0))
    return pl.pallas_call(
        functools.partial(_convmod_kernel, seq=s, ch=ch, tile=tile),
        grid=(b,),
        in_specs=[
            pl.BlockSpec((None, s, c2), lambda i: (i, 0, 0)),
            pl.BlockSpec((None, CONV_WIDTH, ch), lambda i: (layer, 0, 0)),
            vec(), vec(), vec(),
        ],
        out_specs=pl.BlockSpec((None, s, ch), lambda i: (i, 0, 0)),
        out_shape=jax.ShapeDtypeStruct((b, s, ch), BF16),
        scratch_shapes=[pltpu.VMEM((s + 40, ch), F32)],
        compiler_params=pltpu.CompilerParams(dimension_semantics=("arbitrary",), vmem_limit_bytes=VMEM_LIMIT),
        name="convmod",
    )(uconv, dw_w, dw_b, ln_w, ln_b)


def _diffattn_kernel(lam_ref, q_ref, k_ref, v_ref, sw_ref, o_ref, *, tq, seq, slopes, lambda_init):
    p = pl.program_id(1)
    qi = pl.program_id(2)
    lp = lam_ref[...]
    lam = (jnp.exp(jnp.sum(lp[0:1] * lp[1:2], axis=-1, keepdims=True))
           - jnp.exp(jnp.sum(lp[2:3] * lp[3:4], axis=-1, keepdims=True)) + lambda_init)
    row = lax.broadcasted_iota(jnp.int32, (tq, seq), 0) + qi * tq
    col = lax.broadcasted_iota(jnp.int32, (tq, seq), 1)
    dist = jnp.abs(row - col).astype(F32)
    scale = DIFF_DIM ** -0.5
    n_pairs = len(slopes) // 2
    for hh in range(2):
        slope = jnp.float32(slopes[2 * (n_pairs - 1) + hh])
        for pp in range(n_pairs - 2, -1, -1):
            slope = jnp.where(p == pp, jnp.float32(slopes[2 * pp + hh]), slope)
        bias = dist * slope
        vh = v_ref[:, HEAD_DIM * hh:HEAD_DIM * (hh + 1)]
        outs = []
        for j in range(2):
            off = HEAD_DIM * hh + DIFF_DIM * j
            qm = (q_ref[:, off:off + DIFF_DIM].astype(F32) * scale).astype(BF16)
            km = k_ref[:, off:off + DIFF_DIM]
            s = _mm_nt(qm, km) - bias
            m = jnp.max(s, axis=-1, keepdims=True)
            e = jnp.exp(s - m)
            l = jnp.sum(e, axis=-1, keepdims=True)
            outs.append(_mm(e.astype(BF16), vh) / l)
        o = outs[0] - lam * outs[1]
        o = _rms(o, sw_ref[...], 1e-5) * (1.0 - lambda_init)
        o_ref[:, HEAD_DIM * hh:HEAD_DIM * (hh + 1)] = o.astype(o_ref.dtype)


def _diffattn(dqkv, lam_all, subln_w, layer, n_heads, tq):
    b, s, w3 = dqkv.shape
    n_pairs = n_heads // 2
    lambda_init = 0.8 - 0.6 * math.exp(-0.3 * layer)
    slopes = tuple(_alibi_slopes(n_heads))
    return pl.pallas_call(
        functools.partial(_diffattn_kernel, tq=tq, seq=s, slopes=slopes, lambda_init=lambda_init),
        grid=(b, n_pairs, s // tq),
        in_specs=[
            pl.BlockSpec((None, 4, DIFF_DIM), lambda i, p, q: (layer, 0, 0)),
            pl.BlockSpec((None, tq, LANES), lambda i, p, q: (i, q, p)),
            pl.BlockSpec((None, s, LANES), lambda i, p, q: (i, 0, n_pairs + p)),
            pl.BlockSpec((None, s, LANES), lambda i, p, q: (i, 0, 2 * n_pairs + p)),
            pl.BlockSpec((None, 1, HEAD_DIM), lambda i, p, q: (layer, 0, 0)),
        ],
        out_specs=pl.BlockSpec((None, tq, LANES), lambda i, p, q: (i, q, p)),
        out_shape=jax.ShapeDtypeStruct((b, s, n_pairs * LANES), BF16),
        compiler_params=pltpu.CompilerParams(
            dimension_semantics=("arbitrary", "arbitrary", "arbitrary"), vmem_limit_bytes=VMEM_LIMIT),
        name="diffattn",
    )(lam_all, dqkv, dqkv, dqkv, subln_w)


def _unit_tri_inverse(a, ii, jj):
    eye = (ii == jj).astype(F32)
    bd = jnp.where((ii // 16) == (jj // 16), -a, 0.0)
    bdb = bd.astype(BF16)
    p = eye + bd
    q = _mm(bdb, bdb)
    for _ in range(2):
        pq = _mm(q.astype(BF16), jnp.concatenate([p, q], axis=1).astype(BF16))
        p = p + pq[:, :CHUNK]
        q = pq[:, CHUNK:]
    t = p + _mm(q.astype(BF16), p.astype(BF16))
    for blk in (16, 32):
        join = ((ii // (2 * blk)) == (jj // (2 * blk))) & ((ii // blk) != (jj // blk))
        lb = jnp.where(join, a, 0.0).astype(BF16)
        tb = t.astype(BF16)
        t = t - _mm(_mm(tb, lb).astype(BF16), tb)
    return t


def _deltanet_kernel(gq_ref, gk_ref, gv_ref, z_ref, gate_ref, cwq_ref, cwk_ref, cwv_ref, pcol_ref, nw_ref,
                     o_ref,
                     kbf_s, qbf_s, qg_s, kdec_s, rhs_s, gc_s, be_s, gl_s, u_s, w_s, aqk_s, osum_s,
                     *, seq, n_gate_heads):
    p = pl.program_id(1)
    n_chunks = seq // CHUNK
    hd = HEAD_DIM

    row = lax.broadcasted_iota(jnp.int32, (seq, LANES), 0)
    rin = row & (CHUNK - 1)
    il = lax.broadcasted_iota(jnp.int32, (LANES, LANES), 0)
    jl = lax.broadcasted_iota(jnp.int32, (LANES, LANES), 1)
    same_head = ((il // hd) == (jl // hd)).astype(BF16)

    def short_conv(x_ref, w_ref):
        x = x_ref[...].astype(F32)
        xm = jnp.where(row == 0, 0.0, pltpu.roll(x, 1, 0))
        xp = jnp.where(row == seq - 1, 0.0, pltpu.roll(x, seq - 1, 0))
        y = xm * w_ref[0:1, :] + x * w_ref[1:2, :] + xp * w_ref[2:3, :]
        return _silu(y)

    def l2n(x):
        return x * lax.rsqrt(_mm_exact_rhs(x * x, same_head) + 1e-6)

    q = l2n(short_conv(gq_ref, cwq_ref)) * (hd ** -0.5)
    k = l2n(short_conv(gk_ref, cwk_ref))
    v = short_conv(gv_ref, cwv_ref)
    kbf_s[...] = k.astype(BF16)
    qbf_s[...] = q.astype(BF16)

    raw = gate_ref[...]
    beta_all = jax.nn.sigmoid(raw)
    g_all = -jnp.exp(pcol_ref[0:1, :]) * _softplus(raw + pcol_ref[1:2, :])

    def expand(x, base):
        sel = (il == base + 2 * p + (jl // hd)).astype(BF16)
        return _mm_exact_rhs(x, sel)

    def seg_cumsum(x, reverse):
        for sh in (1, 2, 4, 8, 16, 32):
            if reverse:
                x = x + jnp.where(rin < CHUNK - sh, pltpu.roll(x, seq - sh, 0), 0.0)
            else:
                x = x + jnp.where(rin >= sh, pltpu.roll(x, sh, 0), 0.0)
        return x

    for d in range(2):
        be = expand(beta_all, n_gate_heads * d)
        ge = expand(g_all, n_gate_heads * (2 + d))
        fwd = seg_cumsum(ge, False)
        rev = seg_cumsum(ge, True)
        gc = fwd if d == 0 else rev
        rest = (rev if d == 0 else fwd) - ge
        eg = jnp.exp(gc)
        kb = k * be
        vb = (v * be).astype(BF16)
        kbe = (kb * eg).astype(BF16)
        gc_s[d] = gc
        be_s[d] = be
        gl_s[d] = jnp.exp(fwd + rev - ge)
        qg_s[d] = (q * eg).astype(BF16)
        kdec_s[d] = (k * jnp.exp(rest)).astype(BF16)
        rhs_s[d, :, 0 * hd:1 * hd] = vb[:, :hd]
        rhs_s[d, :, 1 * hd:2 * hd] = kbe[:, :hd]
        rhs_s[d, :, 2 * hd:3 * hd] = vb[:, hd:]
        rhs_s[d, :, 3 * hd:4 * hd] = kbe[:, hd:]

    ii = lax.broadcasted_iota(jnp.int32, (CHUNK, CHUNK), 0)
    jj = lax.broadcasted_iota(jnp.int32, (CHUNK, CHUNK), 1)

    def prep(c, carry):
        r0 = pl.multiple_of(c * CHUNK, CHUNK)
        rows = pl.ds(r0, CHUNK)
        for hh in range(2):
            ls = slice(hd * hh, hd * (hh + 1))
            kh = kbf_s[rows, ls]
            qh = qbf_s[rows, ls]
            kk = _mm_nt(kh, kh)
            qk = _mm_nt(qh, kh)
            for d in range(2):
                x = gc_s[d, rows, ls]
                diff = x - x.T
                incl = (ii >= jj) if d == 0 else (ii <= jj)
                strict = (ii > jj) if d == 0 else (ii < jj)
                dec = jnp.exp(jnp.where(incl, diff, -1e30))
                a = jnp.where(strict, kk * dec, 0.0) * be_s[d, rows, ls]
                t = _unit_tri_inverse(a, ii, jj)
                uw = _mm(t.astype(BF16), rhs_s[d, rows, 2 * hd * hh:2 * hd * (hh + 1)])
                idx = 2 * d + hh
                u_s[idx, c] = uw[:, :hd]
                w_s[idx, c] = uw[:, hd:].astype(BF16)
                aqk_s[idx, c] = (qk * dec).astype(BF16)
        return carry

    lax.fori_loop(0, n_chunks, prep, 0)

    def scan(n, states):
        new = []
        for d in range(2):
            c = n if d == 0 else n_chunks - 1 - n
            r0 = pl.multiple_of(c * CHUNK, CHUNK)
            rows = pl.ds(r0, CHUNK)
            for hh in range(2):
                idx = 2 * d + hh
                ls = slice(hd * hh, hd * (hh + 1))
                st = states[idx]
                sb = st.astype(BF16)
                vnb = (u_s[idx, c] - _mm(w_s[idx, c], sb)).astype(BF16)
                osum_s[d, rows, ls] = _mm(qg_s[d, rows, ls], sb) + _mm(aqk_s[idx, c], vnb)
                new.append(st * gl_s[d, rows, ls] + _mm_tn(kdec_s[d, rows, ls], vnb))
        return tuple(new)

    zero = jnp.zeros((hd, hd), F32)
    lax.fori_loop(0, n_chunks, scan, (zero,) * 4)

    o = osum_s[0] + osum_s[1]
    ms = _mm_exact_rhs(o * o, same_head) * (1.0 / hd)
    y = o * lax.rsqrt(ms + NORM_EPS) * nw_ref[...] * _silu(z_ref[...].astype(F32))
    o_ref[...] = y.astype(o_ref.dtype)


def _deltanet(gqkv, gz, gates, conv_w, pcol, norm_w2, layer, n_heads):
    b, s, w3 = gqkv.shape
    n_pairs = n_heads // 2
    n_chunks = s // CHUNK
    blk = lambda off: pl.BlockSpec((None, s, LANES), lambda i, p: (i, 0, off + p))
    cw = lambda off: pl.BlockSpec((None, SHORT_CONV, LANES), lambda i, p: (layer, 0, off + p))
    return pl.pallas_call(
        functools.partial(_deltanet_kernel, seq=s, n_gate_heads=n_heads),
        grid=(b, n_pairs),
        in_specs=[
            blk(0), blk(n_pairs), blk(2 * n_pairs),
            pl.BlockSpec((None, s, LANES), lambda i, p: (i, 0, p)),
            pl.BlockSpec((None, s, LANES), lambda i, p: (i, 0, 0)),
            cw(0), cw(n_pairs), cw(2 * n_pairs),
            pl.BlockSpec((None, 8, LANES), lambda i, p: (layer, 0, 0)),
            pl.BlockSpec((None, 1, LANES), lambda i, p: (layer, 0, 0)),
        ],
        out_specs=pl.BlockSpec((None, s, LANES), lambda i, p: (i, 0, p)),
        out_shape=jax.ShapeDtypeStruct((b, s, n_pairs * LANES), BF16),
        scratch_shapes=[
            pltpu.VMEM((s, LANES), BF16),
            pltpu.VMEM((s, LANES), BF16),
            pltpu.VMEM((2, s, LANES), BF16),
            pltpu.VMEM((2, s, LANES), BF16),
            pltpu.VMEM((2, s, 2 * LANES), BF16),
            pltpu.VMEM((2, s, LANES), F32),
            pltpu.VMEM((2, s, LANES), F32),
            pltpu.VMEM((2, s, LANES), F32),
            pltpu.VMEM((4, n_chunks, CHUNK, HEAD_DIM), F32),
            pltpu.VMEM((4, n_chunks, CHUNK, HEAD_DIM), BF16),
            pltpu.VMEM((4, n_chunks, CHUNK, CHUNK), BF16),
            pltpu.VMEM((2, s, LANES), F32),
        ],
        compiler_params=pltpu.CompilerParams(
            dimension_semantics=("arbitrary", "arbitrary"), vmem_limit_bytes=VMEM_LIMIT),
        name="deltanet",
    )(gqkv, gqkv, gqkv, gz, gates, conv_w, conv_w, conv_w, pcol, norm_w2)


def _out_proj_kernel(yc_ref, yd_ref, yg_ref, w_ref, x_ref, nw_ref, o_ref):
    c0 = yc_ref.shape[-1]
    c1 = c0 + yd_ref.shape[-1]
    y = (_mm(yc_ref[...], w_ref[0:c0, :]) + _mm(yd_ref[...], w_ref[c0:c1, :])
         + _mm(yg_ref[...], w_ref[c1:, :]))
    o_ref[...] = x_ref[...] + _rms(y, nw_ref[...], NORM_EPS)


def _out_proj(yc, yd, yg, w_all, x2, nw, layer, tm):
    m, d = x2.shape
    row = lambda a: pl.BlockSpec((tm, a.shape[-1]), lambda i: (i, 0))
    return pl.pallas_call(
        _out_proj_kernel,
        grid=(m // tm,),
        in_specs=[
            row(yc), row(yd), row(yg),
            pl.BlockSpec((None, d, d), lambda i: (layer, 0, 0)),
            pl.BlockSpec((tm, d), lambda i: (i, 0)),
            pl.BlockSpec((None, 1, d), lambda i: (layer, 0, 0)),
        ],
        out_specs=pl.BlockSpec((tm, d), lambda i: (i, 0)),
        out_shape=jax.ShapeDtypeStruct((m, d), F32),
        compiler_params=pltpu.CompilerParams(dimension_semantics=("arbitrary",), vmem_limit_bytes=VMEM_LIMIT),
        name="out_proj",
    )(yc, yd, yg, w_all, x2, nw)


def _ffn_kernel(x_ref, nw1_ref, w1_ref, w2_ref, nw2_ref, o_ref, hb_s, acc_s):
    j = pl.program_id(1)

    @pl.when(j == 0)
    def _():
        hb_s[...] = _rms(x_ref[...], nw1_ref[...], NORM_EPS).astype(BF16)

    a = _mm(hb_s[...], w1_ref[...])
    a = jnp.square(jnp.maximum(a, 0.0)).astype(BF16)
    part = _mm(a, w2_ref[...])

    @pl.when(j == 0)
    def _():
        acc_s[...] = part

    @pl.when(j > 0)
    def _():
        acc_s[...] = acc_s[...] + part

    @pl.when(j == pl.num_programs(1) - 1)
    def _():
        o_ref[...] = x_ref[...] + _rms(acc_s[...], nw2_ref[...], NORM_EPS)


def _ffn(x2, nw1, w1_all, w2_all, nw2, layer, tm, tf):
    m, d = x2.shape
    dff = w1_all.shape[-1]
    return pl.pallas_call(
        _ffn_kernel,
        grid=(m // tm, dff // tf),
        in_specs=[
            pl.BlockSpec((tm, d), lambda i, j: (i, 0)),
            pl.BlockSpec((None, 1, d), lambda i, j: (layer, 0, 0)),
            pl.BlockSpec((None, d, tf), lambda i, j: (layer, 0, j)),
            pl.BlockSpec((None, tf, d), lambda i, j: (layer, j, 0)),
            pl.BlockSpec((None, 1, d), lambda i, j: (layer, 0, 0)),
        ],
        out_specs=pl.BlockSpec((tm, d), lambda i, j: (i, 0)),
        out_shape=jax.ShapeDtypeStruct((m, d), F32),
        scratch_shapes=[pltpu.VMEM((tm, d), BF16), pltpu.VMEM((tm, d), F32)],
        compiler_params=pltpu.CompilerParams(
            dimension_semantics=("arbitrary", "arbitrary"), vmem_limit_bytes=VMEM_LIMIT),
        name="ffn",
    )(x2, nw1, w1_all, w2_all, nw2)


def kernel(x, w_in, w_out, pre_mix_w, post_mix_w, pre_mlp_w, post_mlp_w, w_ff1, w_ff2, conv_dw_w, conv_dw_b, conv_ln_w, conv_ln_b, diff_lambda_q1, diff_lambda_k1, diff_lambda_q2, diff_lambda_k2, diff_subln_w, delta_conv_w, delta_A_log, delta_dt_bias, delta_norm_w):
    b, s, d = x.shape
    depth = w_in.shape[0]
    conv_ch = conv_dw_w.shape[-1]
    delta_w = delta_conv_w.shape[-1] // 3
    n_delta_heads = delta_w // HEAD_DIM
    in_w = w_in.shape[-1]
    diff_w = (in_w - 2 * conv_ch - 4 * delta_w - 4 * n_delta_heads) // 3
    n_diff_heads = diff_w // HEAD_DIM
    n_gates = 4 * n_delta_heads
    splits = (2 * conv_ch, 3 * diff_w, 3 * delta_w, delta_w, LANES)
    m = b * s
    tm = min(512, m)

    w_in_b = jnp.pad(w_in, ((0, 0), (0, 0), (0, LANES - n_gates))).astype(BF16)
    w_out_b = w_out.astype(BF16)
    w1_b = w_ff1.astype(BF16)
    w2_b = w_ff2.astype(BF16)
    row3 = lambda a: a.reshape(depth, 1, a.shape[-1])
    lam_all = jnp.stack([diff_lambda_q1, diff_lambda_k1, diff_lambda_q2, diff_lambda_k2], axis=1)
    gate_par = jnp.zeros((depth, 8, LANES), F32)
    gate_par = gate_par.at[:, 0, 2 * n_delta_heads:n_gates].set(delta_A_log.reshape(depth, -1))
    gate_par = gate_par.at[:, 1, 2 * n_delta_heads:n_gates].set(delta_dt_bias.reshape(depth, -1))
    delta_nw2 = row3(jnp.concatenate([delta_norm_w, delta_norm_w], axis=-1))

    x2 = x.reshape(m, d)
    for l in range(depth):
        uconv, dqkv, gqkv, gz, gates = _in_proj(x2, row3(pre_mix_w), w_in_b, l, splits, tm)
        y_conv = _convmod(uconv.reshape(b, s, -1), conv_dw_w, row3(conv_dw_b), row3(conv_ln_w),
                          row3(conv_ln_b), l)
        y_diff = _diffattn(dqkv.reshape(b, s, -1), lam_all, row3(diff_subln_w), l, n_diff_heads, min(256, s))
        y_delta = _deltanet(gqkv.reshape(b, s, -1), gz.reshape(b, s, -1), gates.reshape(b, s, -1),
                            delta_conv_w, gate_par, delta_nw2, l, n_delta_heads)
        x2 = _out_proj(y_conv.reshape(m, -1), y_diff.reshape(m, -1), y_delta.reshape(m, -1),
                       w_out_b, x2, row3(post_mix_w), l, tm)
        x2 = _ffn(x2, row3(pre_mlp_w), w1_b, w2_b, row3(post_mlp_w), l, min(1024, m), 512)
    return x2.reshape(b, s, d)
```

```python
import functools
import math

import jax
import jax.numpy as jnp
from jax import lax
from jax.experimental import pallas as pl
from jax.experimental.pallas import tpu as pltpu

F32 = jnp.float32
BF16 = jnp.bfloat16

HEAD_DIM = 64
DIFF_DIM = 32
CONV_WIDTH = 31
SHORT_CONV = 3
CHUNK = 64
NORM_EPS = 1e-6
LANES = 128
GATE_ROWS = 32
VMEM_LIMIT = 56 * 1024 * 1024


def _alibi_slopes(n):
    def pow2(m):
        start = 2.0 ** (-8.0 / m)
        return [start ** (i + 1) for i in range(m)]
    if math.log2(n).is_integer():
        return pow2(n)
    c = 2 ** int(math.floor(math.log2(n)))
    return pow2(c) + pow2(2 * c)[0::2][: n - c]


def _rms(x, w, eps):
    return x * lax.rsqrt(jnp.mean(x * x, axis=-1, keepdims=True) + eps) * w


def _mm(a, b):
    return jnp.dot(a, b, preferred_element_type=F32)


def _mm_nt(a, b):
    return lax.dot_general(a, b, (((1,), (1,)), ((), ())), preferred_element_type=F32)


def _mm_tn(a, b):
    return lax.dot_general(a, b, (((0,), (0,)), ((), ())), preferred_element_type=F32)


def _mm_exact_rhs(x, w):
    hi = x.astype(BF16)
    r1 = x - hi.astype(F32)
    mid = r1.astype(BF16)
    lo = (r1 - mid.astype(F32)).astype(BF16)
    return _mm(hi, w) + _mm(mid, w) + _mm(lo, w)


def _softplus(x):
    return jnp.maximum(x, 0.0) + jnp.log1p(jnp.exp(-jnp.abs(x)))


def _silu(x):
    h = 0.5 * x
    return h + h * jnp.tanh(h)


def _in_proj_kernel(x_ref, nw_ref, w_ref, uconv_ref, dqkv_ref, gqkv_ref, gz_ref, gate_ref, *, splits):
    hb = _rms(x_ref[...], nw_ref[...], NORM_EPS).astype(BF16)
    outs = (uconv_ref, dqkv_ref, gqkv_ref, gz_ref, gate_ref)
    off = 0
    for o_ref, width in zip(outs, splits):
        o_ref[...] = _mm(hb, w_ref[:, off:off + width]).astype(o_ref.dtype)
        off += width


def _in_proj(x2, nw, w_all, layer, splits, tm):
    m, d = x2.shape
    npad = w_all.shape[-1]
    dts = (BF16, BF16, BF16, BF16, F32)
    return pl.pallas_call(
        functools.partial(_in_proj_kernel, splits=splits),
        grid=(m // tm,),
        in_specs=[
            pl.BlockSpec((tm, d), lambda i: (i, 0)),
            pl.BlockSpec((None, 1, d), lambda i: (layer, 0, 0)),
            pl.BlockSpec((None, d, npad), lambda i: (layer, 0, 0)),
        ],
        out_specs=[pl.BlockSpec((tm, w), lambda i: (i, 0)) for w in splits],
        out_shape=[jax.ShapeDtypeStruct((m, w), dt) for w, dt in zip(splits, dts)],
        compiler_params=pltpu.CompilerParams(dimension_semantics=("arbitrary",), vmem_limit_bytes=VMEM_LIMIT),
        name="in_proj",
    )(x2, nw, w_all)


def _convmod_kernel(u_ref, w_ref, b_ref, lnw_ref, lnb_ref, o_ref, pad_ref, *, seq, ch, tile):
    front = 16
    u = u_ref[...].astype(F32)
    h = u[:, :ch] * jax.nn.sigmoid(u[:, ch:])
    pad_ref[0:front, :] = jnp.zeros((front, ch), F32)
    pad_ref[front:front + seq, :] = h
    tail = pad_ref.shape[0] - front - seq
    pad_ref[front + seq:, :] = jnp.zeros((tail, ch), F32)
    win_rows = tile + 40
    half = (CONV_WIDTH - 1) // 2

    def body(t, carry):
        base = pl.multiple_of(t * tile, tile)
        win = pad_ref[pl.ds(base, win_rows), :]
        acc = jnp.zeros((tile, ch), F32)
        for r in range(8):
            wr = win[r:r + tile + 32]
            for a in range(4):
                k = 8 * a + r - (front - half)
                if 0 <= k < CONV_WIDTH:
                    acc = acc + wr[8 * a:8 * a + tile] * w_ref[k:k + 1, :]
        acc = acc + b_ref[...]
        mu = jnp.mean(acc, axis=-1, keepdims=True)
        xc = acc - mu
        var = jnp.mean(xc * xc, axis=-1, keepdims=True)
        y = xc * lax.rsqrt(var + 1e-5) * lnw_ref[...] + lnb_ref[...]
        o_ref[pl.ds(base, tile), :] = _silu(y).astype(o_ref.dtype)
        return carry

    lax.fori_loop(0, seq // tile, body, 0)


def _convmod(uconv, dw_w, dw_b, ln_w, ln_b, layer):
    b, s, c2 = uconv.shape
    ch = c2 // 2
    tile = 64
    vec = lambda: pl.BlockSpec((None, 1, ch), lambda i: (layer, 0, 0))
    return pl.pallas_call(
        functools.partial(_convmod_kernel, seq=s, ch=ch, tile=tile),
        grid=(b,),
        in_specs=[
            pl.BlockSpec((None, s, c2), lambda i: (i, 0, 0)),
            pl.BlockSpec((None, CONV_WIDTH, ch), lambda i: (layer, 0, 0)),
            vec(), vec(), vec(),
        ],
        out_specs=pl.BlockSpec((None, s, ch), lambda i: (i, 0, 0)),
        out_shape=jax.ShapeDtypeStruct((b, s, ch), BF16),
        scratch_shapes=[pltpu.VMEM((s + 40, ch), F32)],
        compiler_params=pltpu.CompilerParams(dimension_semantics=("arbitrary",), vmem_limit_bytes=VMEM_LIMIT),
        name="convmod",
    )(uconv, dw_w, dw_b, ln_w, ln_b)


def _diffattn_kernel(lam_ref, q_ref, k_ref, v_ref, sw_ref, o_ref, *, tq, seq, slopes, lambda_init):
    p = pl.program_id(1)
    qi = pl.program_id(2)
    lp = lam_ref[...]
    lam = (jnp.exp(jnp.sum(lp[0:1] * lp[1:2], axis=-1, keepdims=True))
           - jnp.exp(jnp.sum(lp[2:3] * lp[3:4], axis=-1, keepdims=True)) + lambda_init)
    row = lax.broadcasted_iota(jnp.int32, (tq, seq), 0) + qi * tq
    col = lax.broadcasted_iota(jnp.int32, (tq, seq), 1)
    dist = jnp.abs(row - col).astype(F32)
    scale = DIFF_DIM ** -0.5
    n_pairs = len(slopes) // 2
    for hh in range(2):
        slope = jnp.float32(slopes[2 * (n_pairs - 1) + hh])
        for pp in range(n_pairs - 2, -1, -1):
            slope = jnp.where(p == pp, jnp.float32(slopes[2 * pp + hh]), slope)
        bias = dist * slope
        vh = v_ref[:, HEAD_DIM * hh:HEAD_DIM * (hh + 1)]
        outs = []
        for j in range(2):
            off = HEAD_DIM * hh + DIFF_DIM * j
            qm = (q_ref[:, off:off + DIFF_DIM].astype(F32) * scale).astype(BF16)
            km = k_ref[:, off:off + DIFF_DIM]
            s = _mm_nt(qm, km) - bias
            m = jnp.max(s, axis=-1, keepdims=True)
            e = jnp.exp(s - m)
            l = jnp.sum(e, axis=-1, keepdims=True)
            outs.append(_mm(e.astype(BF16), vh) / l)
        o = outs[0] - lam * outs[1]
        o = _rms(o, sw_ref[...], 1e-5) * (1.0 - lambda_init)
        o_ref[:, HEAD_DIM * hh:HEAD_DIM * (hh + 1)] = o.astype(o_ref.dtype)


def _diffattn(dqkv, lam_all, subln_w, layer, n_heads, tq):
    b, s, w3 = dqkv.shape
    n_pairs = n_heads // 2
    lambda_init = 0.8 - 0.6 * math.exp(-0.3 * layer)
    slopes = tuple(_alibi_slopes(n_heads))
    return pl.pallas_call(
        functools.partial(_diffattn_kernel, tq=tq, seq=s, slopes=slopes, lambda_init=lambda_init),
        grid=(b, n_pairs, s // tq),
        in_specs=[
            pl.BlockSpec((None, 4, DIFF_DIM), lambda i, p, q: (layer, 0, 0)),
            pl.BlockSpec((None, tq, LANES), lambda i, p, q: (i, q, p)),
            pl.BlockSpec((None, s, LANES), lambda i, p, q: (i, 0, n_pairs + p)),
            pl.BlockSpec((None, s, LANES), lambda i, p, q: (i, 0, 2 * n_pairs + p)),
            pl.BlockSpec((None, 1, HEAD_DIM), lambda i, p, q: (layer, 0, 0)),
        ],
        out_specs=pl.BlockSpec((None, tq, LANES), lambda i, p, q: (i, q, p)),
        out_shape=jax.ShapeDtypeStruct((b, s, n_pairs * LANES), BF16),
        compiler_params=pltpu.CompilerParams(
            dimension_semantics=("arbitrary", "arbitrary", "arbitrary"), vmem_limit_bytes=VMEM_LIMIT),
        name="diffattn",
    )(lam_all, dqkv, dqkv, dqkv, subln_w)


def _unit_tri_inverses(mats, ii, jj):
    eye = (ii == jj).astype(F32)
    diag16 = (ii // 16) == (jj // 16)
    bds = [jnp.where(diag16, -a, 0.0) for a in mats]
    bdb = [x.astype(BF16) for x in bds]
    ps = [eye + x for x in bds]
    qs = [_mm(x, x) for x in bdb]
    for _ in range(2):
        pqs = [_mm(q.astype(BF16), jnp.concatenate([p, q], axis=1).astype(BF16)) for p, q in zip(ps, qs)]
        ps = [p + pq[:, :CHUNK] for p, pq in zip(ps, pqs)]
        qs = [pq[:, CHUNK:] for pq in pqs]
    ts = [p + _mm(q.astype(BF16), p.astype(BF16)) for p, q in zip(ps, qs)]
    for blk in (16, 32):
        join = ((ii // (2 * blk)) == (jj // (2 * blk))) & ((ii // blk) != (jj // blk))
        lbs = [jnp.where(join, a, 0.0).astype(BF16) for a in mats]
        tbs = [t.astype(BF16) for t in ts]
        tls = [_mm(tb, lb).astype(BF16) for tb, lb in zip(tbs, lbs)]
        ts = [t - _mm(tl, tb) for t, tl, tb in zip(ts, tls, tbs)]
    return ts


def _deltanet_kernel(gq_ref, gk_ref, gv_ref, z_ref, gate_ref, cwq_ref, cwk_ref, cwv_ref, gpar_ref, nw_ref,
                     o_ref,
                     pad_s, kbf_s, qbf_s, qg_s, kdec_s, rhs_s, gc_s, be_s, gl_s, cc_s, wp_s, op_s, qp_s, osum_s,
                     *, seq, n_gate_heads, group_chunks):
    p = pl.program_id(1)
    n_chunks = seq // CHUNK
    hd = HEAD_DIM

    il = lax.broadcasted_iota(jnp.int32, (LANES, LANES), 0)
    jl = lax.broadcasted_iota(jnp.int32, (LANES, LANES), 1)
    same_head = ((il // hd) == (jl // hd)).astype(BF16)

    edge = jnp.zeros((8, LANES), F32)
    pad_s[0:8, :] = edge
    pad_s[8 + seq:, :] = edge

    def short_conv(x_ref, w_ref):
        pad_s[8:8 + seq, :] = x_ref[...].astype(F32)
        y = (pad_s[7:7 + seq, :] * w_ref[0:1, :] + pad_s[8:8 + seq, :] * w_ref[1:2, :]
             + pad_s[9:9 + seq, :] * w_ref[2:3, :])
        return _silu(y)

    def l2n(x):
        return x * lax.rsqrt(_mm_exact_rhs(x * x, same_head) + 1e-6)

    q = l2n(short_conv(gq_ref, cwq_ref)) * (hd ** -0.5)
    k = l2n(short_conv(gk_ref, cwk_ref))
    v = short_conv(gv_ref, cwv_ref)
    kbf_s[...] = k.astype(BF16)
    qbf_s[...] = q.astype(BF16)

    nh = n_gate_heads
    raw = gate_ref[...].T[0:GATE_ROWS]
    reps = seq // LANES
    a_log = pltpu.repeat(gpar_ref[0], reps, axis=1)
    dt_bias = pltpu.repeat(gpar_ref[1], reps, axis=1)
    beta = jax.nn.sigmoid(raw)
    g = -jnp.exp(a_log) * _softplus(raw + dt_bias)
    gate_row = lax.broadcasted_iota(jnp.int32, (GATE_ROWS, seq), 0)
    pos_in_chunk = lax.broadcasted_iota(jnp.int32, (GATE_ROWS, seq), 1) & (CHUNK - 1)
    fwd, rev = g, g
    for sh in (1, 2, 4, 8, 16, 32):
        fwd = fwd + jnp.where(pos_in_chunk >= sh, pltpu.roll(fwd, sh, 1), 0.0)
        rev = rev + jnp.where(pos_in_chunk < CHUNK - sh, pltpu.roll(rev, seq - sh, 1), 0.0)
    gc_r = jnp.where(gate_row < 3 * nh, fwd, rev)

    ir = lax.broadcasted_iota(jnp.int32, (GATE_ROWS, 2 * LANES), 0)
    jr = lax.broadcasted_iota(jnp.int32, (GATE_ROWS, 2 * LANES), 1)

    def expand(x, base, pieces):
        sel = (ir == base + nh * (jr // LANES) + 2 * p + (jr % LANES) // hd).astype(BF16)
        out = None
        for _ in range(pieces):
            xb = x.astype(BF16)
            part = _mm_tn(xb, sel)
            out = part if out is None else out + part
            x = x - xb.astype(F32)
        return out

    be_all = expand(beta, 0, 2)
    gc_all = expand(gc_r, 2 * nh, 3)

    for d in range(2):
        dl = slice(LANES * d, LANES * (d + 1))
        be = be_all[:, dl]
        gc = gc_all[:, dl]
        gc3 = gc.reshape(n_chunks, CHUNK, LANES)
        last = CHUNK - 1 if d == 0 else 0
        total3 = jnp.broadcast_to(gc3[:, last:last + 1, :], gc3.shape)
        eg = jnp.exp(gc)
        kb = k * be
        vb = (v * be).astype(BF16)
        kbe = (kb * eg).astype(BF16)
        gc_s[d] = gc
        be_s[d] = be
        gl_s[d] = jnp.exp(total3).reshape(seq, LANES)
        qg_s[d] = (q * eg).astype(BF16)
        kdec_s[d] = (k * jnp.exp(total3 - gc3).reshape(seq, LANES)).astype(BF16)
        rhs_s[d, :, 0 * hd:1 * hd] = vb[:, :hd]
        rhs_s[d, :, 1 * hd:2 * hd] = kbe[:, :hd]
        rhs_s[d, :, 2 * hd:3 * hd] = vb[:, hd:]
        rhs_s[d, :, 3 * hd:4 * hd] = kbe[:, hd:]

    ii = lax.broadcasted_iota(jnp.int32, (CHUNK, CHUNK), 0)
    jj = lax.broadcasted_iota(jnp.int32, (CHUNK, CHUNK), 1)

    def prep(g, carry):
        keys, a_list, aqk_list = [], [], []
        for ci in range(group_chunks):
            c = g * group_chunks + ci
            rows = pl.ds(pl.multiple_of(c * CHUNK, CHUNK), CHUNK)
            for hh in range(2):
                ls = slice(hd * hh, hd * (hh + 1))
                kh = kbf_s[rows, ls]
                kq = _mm_nt(jnp.concatenate([kh, qbf_s[rows, ls]], axis=0), kh)
                kk, qk = kq[:CHUNK], kq[CHUNK:]
                for d in range(2):
                    x = gc_s[d, rows, ls]
                    diff = x - x.T
                    incl = (ii >= jj) if d == 0 else (ii <= jj)
                    strict = (ii > jj) if d == 0 else (ii < jj)
                    dec = jnp.exp(jnp.where(incl, diff, -1e30))
                    a_list.append(jnp.where(strict, kk * dec, 0.0) * be_s[d, rows, ls])
                    aqk_list.append((qk * dec).astype(BF16))
                    keys.append((c, rows, hh, d, ls))
        ts = _unit_tri_inverses(a_list, ii, jj)
        uws = [_mm(t.astype(BF16), rhs_s[d, rows, 2 * hd * hh:2 * hd * (hh + 1)]).astype(BF16)
               for t, (c, rows, hh, d, ls) in zip(ts, keys)]
        kds = [_mm_tn(kdec_s[d, rows, ls], uw) for uw, (c, rows, hh, d, ls) in zip(uws, keys)]
        aos = [_mm(aqk, uw) for aqk, uw in zip(aqk_list, uws)]
        for kd, ao, (c, rows, hh, d, ls) in zip(kds, aos, keys):
            idx = 2 * d + hh
            cc_s[idx, c] = kd[:, :hd]
            wp_s[idx, c] = kd[:, hd:].astype(BF16)
            op_s[idx, c] = ao[:, :hd]
            qp_s[idx, c] = (qg_s[d, rows, ls].astype(F32) - ao[:, hd:]).astype(BF16)
        return carry

    lax.fori_loop(0, n_chunks // group_chunks, prep, 0)

    def scan(n, states):
        new = []
        for d in range(2):
            c = n if d == 0 else n_chunks - 1 - n
            rows = pl.ds(pl.multiple_of(c * CHUNK, CHUNK), CHUNK)
            for hh in range(2):
                idx = 2 * d + hh
                ls = slice(hd * hh, hd * (hh + 1))
                st = states[idx]
                sb = st.astype(BF16)
                osum_s[d, rows, ls] = _mm(qp_s[idx, c], sb) + op_s[idx, c]
                new.append(st * gl_s[d, rows, ls] - _mm(wp_s[idx, c], sb) + cc_s[idx, c])
        return tuple(new)

    zero = jnp.zeros((hd, hd), F32)
    lax.fori_loop(0, n_chunks, scan, (zero,) * 4)

    o = osum_s[0] + osum_s[1]
    ms = _mm_exact_rhs(o * o, same_head) * (1.0 / hd)
    y = o * lax.rsqrt(ms + NORM_EPS) * nw_ref[...] * _silu(z_ref[...].astype(F32))
    o_ref[...] = y.astype(o_ref.dtype)


def _deltanet(gqkv, gz, gates, conv_w, gpar, norm_w2, layer, n_heads):
    b, s, w3 = gqkv.shape
    n_pairs = n_heads // 2
    n_chunks = s // CHUNK
    blk = lambda off: pl.BlockSpec((None, s, LANES), lambda i, p: (i, 0, off + p))
    cw = lambda off: pl.BlockSpec((None, SHORT_CONV, LANES), lambda i, p: (layer, 0, off + p))
    return pl.pallas_call(
        functools.partial(_deltanet_kernel, seq=s, n_gate_heads=n_heads, group_chunks=min(4, n_chunks)),
        grid=(b, n_pairs),
        in_specs=[
            blk(0), blk(n_pairs), blk(2 * n_pairs),
            pl.BlockSpec((None, s, LANES), lambda i, p: (i, 0, p)),
            pl.BlockSpec((None, s, LANES), lambda i, p: (i, 0, 0)),
            cw(0), cw(n_pairs), cw(2 * n_pairs),
            pl.BlockSpec((None, 2, GATE_ROWS, LANES), lambda i, p: (layer, 0, 0, 0)),
            pl.BlockSpec((None, 1, LANES), lambda i, p: (layer, 0, 0)),
        ],
        out_specs=pl.BlockSpec((None, s, LANES), lambda i, p: (i, 0, p)),
        out_shape=jax.ShapeDtypeStruct((b, s, n_pairs * LANES), BF16),
        scratch_shapes=[
            pltpu.VMEM((s + 16, LANES), F32),
            pltpu.VMEM((s, LANES), BF16),
            pltpu.VMEM((s, LANES), BF16),
            pltpu.VMEM((2, s, LANES), BF16),
            pltpu.VMEM((2, s, LANES), BF16),
            pltpu.VMEM((2, s, 2 * LANES), BF16),
            pltpu.VMEM((2, s, LANES), F32),
            pltpu.VMEM((2, s, LANES), F32),
            pltpu.VMEM((2, s, LANES), F32),
            pltpu.VMEM((4, n_chunks, HEAD_DIM, HEAD_DIM), F32),
            pltpu.VMEM((4, n_chunks, HEAD_DIM, HEAD_DIM), BF16),
            pltpu.VMEM((4, n_chunks, CHUNK, HEAD_DIM), F32),
            pltpu.VMEM((4, n_chunks, CHUNK, HEAD_DIM), BF16),
            pltpu.VMEM((2, s, LANES), F32),
        ],
        compiler_params=pltpu.CompilerParams(
            dimension_semantics=("arbitrary", "arbitrary"), vmem_limit_bytes=VMEM_LIMIT),
        name="deltanet",
    )(gqkv, gqkv, gqkv, gz, gates, conv_w, conv_w, conv_w, gpar, norm_w2)


def _out_proj_kernel(yc_ref, yd_ref, yg_ref, w_ref, x_ref, nw_ref, o_ref):
    c0 = yc_ref.shape[-1]
    c1 = c0 + yd_ref.shape[-1]
    y = (_mm(yc_ref[...], w_ref[0:c0, :]) + _mm(yd_ref[...], w_ref[c0:c1, :])
         + _mm(yg_ref[...], w_ref[c1:, :]))
    o_ref[...] = x_ref[...] + _rms(y, nw_ref[...], NORM_EPS)


def _out_proj(yc, yd, yg, w_all, x2, nw, layer, tm):
    m, d = x2.shape
    row = lambda a: pl.BlockSpec((tm, a.shape[-1]), lambda i: (i, 0))
    return pl.pallas_call(
        _out_proj_kernel,
        grid=(m // tm,),
        in_specs=[
            row(yc), row(yd), row(yg),
            pl.BlockSpec((None, d, d), lambda i: (layer, 0, 0)),
            pl.BlockSpec((tm, d), lambda i: (i, 0)),
            pl.BlockSpec((None, 1, d), lambda i: (layer, 0, 0)),
        ],
        out_specs=pl.BlockSpec((tm, d), lambda i: (i, 0)),
        out_shape=jax.ShapeDtypeStruct((m, d), F32),
        compiler_params=pltpu.CompilerParams(dimension_semantics=("arbitrary",), vmem_limit_bytes=VMEM_LIMIT),
        name="out_proj",
    )(yc, yd, yg, w_all, x2, nw)


def _ffn_kernel(x_ref, nw1_ref, w1_ref, w2_ref, nw2_ref, o_ref, hb_s, acc_s):
    j = pl.program_id(1)

    @pl.when(j == 0)
    def _():
        hb_s[...] = _rms(x_ref[...], nw1_ref[...], NORM_EPS).astype(BF16)

    a = _mm(hb_s[...], w1_ref[...])
    a = jnp.square(jnp.maximum(a, 0.0)).astype(BF16)
    part = _mm(a, w2_ref[...])

    @pl.when(j == 0)
    def _():
        acc_s[...] = part

    @pl.when(j > 0)
    def _():
        acc_s[...] = acc_s[...] + part

    @pl.when(j == pl.num_programs(1) - 1)
    def _():
        o_ref[...] = x_ref[...] + _rms(acc_s[...], nw2_ref[...], NORM_EPS)


def _ffn(x2, nw1, w1_all, w2_all, nw2, layer, tm, tf):
    m, d = x2.shape
    dff = w1_all.shape[-1]
    return pl.pallas_call(
        _ffn_kernel,
        grid=(m // tm, dff // tf),
        in_specs=[
            pl.BlockSpec((tm, d), lambda i, j: (i, 0)),
            pl.BlockSpec((None, 1, d), lambda i, j: (layer, 0, 0)),
            pl.BlockSpec((None, d, tf), lambda i, j: (layer, 0, j)),
            pl.BlockSpec((None, tf, d), lambda i, j: (layer, j, 0)),
            pl.BlockSpec((None, 1, d), lambda i, j: (layer, 0, 0)),
        ],
        out_specs=pl.BlockSpec((tm, d), lambda i, j: (i, 0)),
        out_shape=jax.ShapeDtypeStruct((m, d), F32),
        scratch_shapes=[pltpu.VMEM((tm, d), BF16), pltpu.VMEM((tm, d), F32)],
        compiler_params=pltpu.CompilerParams(
            dimension_semantics=("arbitrary", "arbitrary"), vmem_limit_bytes=VMEM_LIMIT),
        name="ffn",
    )(x2, nw1, w1_all, w2_all, nw2)


def kernel(x, w_in, w_out, pre_mix_w, post_mix_w, pre_mlp_w, post_mlp_w, w_ff1, w_ff2, conv_dw_w, conv_dw_b, conv_ln_w, conv_ln_b, diff_lambda_q1, diff_lambda_k1, diff_lambda_q2, diff_lambda_k2, diff_subln_w, delta_conv_w, delta_A_log, delta_dt_bias, delta_norm_w):
    b, s, d = x.shape
    depth = w_in.shape[0]
    conv_ch = conv_dw_w.shape[-1]
    delta_w = delta_conv_w.shape[-1] // 3
    n_delta_heads = delta_w // HEAD_DIM
    in_w = w_in.shape[-1]
    diff_w = (in_w - 2 * conv_ch - 4 * delta_w - 4 * n_delta_heads) // 3
    n_diff_heads = diff_w // HEAD_DIM
    n_gates = 4 * n_delta_heads
    splits = (2 * conv_ch, 3 * diff_w, 3 * delta_w, delta_w, LANES)
    m = b * s
    tm = min(512, m)

    w_in_b = jnp.pad(w_in, ((0, 0), (0, 0), (0, LANES - n_gates))).astype(BF16)
    w_out_b = w_out.astype(BF16)
    w1_b = w_ff1.astype(BF16)
    w2_b = w_ff2.astype(BF16)
    row3 = lambda a: a.reshape(depth, 1, a.shape[-1])
    lam_all = jnp.stack([diff_lambda_q1, diff_lambda_k1, diff_lambda_q2, diff_lambda_k2], axis=1)
    gate_par = jnp.stack([delta_A_log.reshape(depth, -1), delta_dt_bias.reshape(depth, -1)], axis=1)
    gate_par = jnp.pad(gate_par, ((0, 0), (0, 0), (2 * n_delta_heads, GATE_ROWS - n_gates)))
    gate_par = jnp.broadcast_to(gate_par[..., None], (depth, 2, GATE_ROWS, LANES))
    delta_nw2 = row3(jnp.concatenate([delta_norm_w, delta_norm_w], axis=-1))

    x2 = x.reshape(m, d)
    for l in range(depth):
        uconv, dqkv, gqkv, gz, gates = _in_proj(x2, row3(pre_mix_w), w_in_b, l, splits, tm)
        y_conv = _convmod(uconv.reshape(b, s, -1), conv_dw_w, row3(conv_dw_b), row3(conv_ln_w),
                          row3(conv_ln_b), l)
        y_diff = _diffattn(dqkv.reshape(b, s, -1), lam_all, row3(diff_subln_w), l, n_diff_heads, min(256, s))
        y_delta = _deltanet(gqkv.reshape(b, s, -1), gz.reshape(b, s, -1), gates.reshape(b, s, -1),
                            delta_conv_w, gate_par, delta_nw2, l, n_delta_heads)
        x2 = _out_proj(y_conv.reshape(m, -1), y_diff.reshape(m, -1), y_delta.reshape(m, -1),
                       w_out_b, x2, row3(post_mix_w), l, tm)
        x2 = _ffn(x2, row3(pre_mlp_w), w1_b, w2_b, row3(post_mlp_w), l, min(1024, m), 512)
    return x2.reshape(b, s, d)
```

```python
import functools
import math

import jax
import jax.numpy as jnp
from jax import lax
from jax.experimental import pallas as pl
from jax.experimental.pallas import tpu as pltpu

F32 = jnp.float32
BF16 = jnp.bfloat16

HEAD_DIM = 64
DIFF_DIM = 32
CONV_WIDTH = 31
SHORT_CONV = 3
CHUNK = 64
NORM_EPS = 1e-6
LANES = 128
GATE_ROWS = 32
VMEM_LIMIT = 56 * 1024 * 1024


def _alibi_slopes(n):
    def pow2(m):
        start = 2.0 ** (-8.0 / m)
        return [start ** (i + 1) for i in range(m)]
    if math.log2(n).is_integer():
        return pow2(n)
    c = 2 ** int(math.floor(math.log2(n)))
    return pow2(c) + pow2(2 * c)[0::2][: n - c]


def _rms(x, w, eps):
    return x * lax.rsqrt(jnp.mean(x * x, axis=-1, keepdims=True) + eps) * w


def _mm(a, b):
    return jnp.dot(a, b, preferred_element_type=F32)


def _mm_nt(a, b):
    return lax.dot_general(a, b, (((1,), (1,)), ((), ())), preferred_element_type=F32)


def _mm_tn(a, b):
    return lax.dot_general(a, b, (((0,), (0,)), ((), ())), preferred_element_type=F32)


def _mm_exact_rhs(x, w):
    hi = x.astype(BF16)
    r1 = x - hi.astype(F32)
    mid = r1.astype(BF16)
    lo = (r1 - mid.astype(F32)).astype(BF16)
    return _mm(hi, w) + _mm(mid, w) + _mm(lo, w)


def _softplus(x):
    return jnp.maximum(x, 0.0) + jnp.log1p(jnp.exp(-jnp.abs(x)))


def _silu(x):
    h = 0.5 * x
    return h + h * jnp.tanh(h)


def _in_proj_kernel(x_ref, nw_ref, w_ref, uconv_ref, dqkv_ref, gqkv_ref, gz_ref, gate_ref, *, splits):
    hb = _rms(x_ref[...], nw_ref[...], NORM_EPS).astype(BF16)
    outs = (uconv_ref, dqkv_ref, gqkv_ref, gz_ref, gate_ref)
    off = 0
    for o_ref, width in zip(outs, splits):
        o_ref[...] = _mm(hb, w_ref[:, off:off + width]).astype(o_ref.dtype)
        off += width


def _in_proj(x2, nw, w_all, layer, splits, tm):
    m, d = x2.shape
    npad = w_all.shape[-1]
    dts = (BF16, BF16, BF16, BF16, F32)
    return pl.pallas_call(
        functools.partial(_in_proj_kernel, splits=splits),
        grid=(m // tm,),
        in_specs=[
            pl.BlockSpec((tm, d), lambda i: (i, 0)),
            pl.BlockSpec((None, 1, d), lambda i: (layer, 0, 0)),
            pl.BlockSpec((None, d, npad), lambda i: (layer, 0, 0)),
        ],
        out_specs=[pl.BlockSpec((tm, w), lambda i: (i, 0)) for w in splits],
        out_shape=[jax.ShapeDtypeStruct((m, w), dt) for w, dt in zip(splits, dts)],
        compiler_params=pltpu.CompilerParams(dimension_semantics=("arbitrary",), vmem_limit_bytes=VMEM_LIMIT),
        name="in_proj",
    )(x2, nw, w_all)


def _convmod_kernel(u_ref, w_ref, b_ref, lnw_ref, lnb_ref, o_ref, pad_ref, *, seq, ch, tile):
    front = 16
    u = u_ref[...].astype(F32)
    h = u[:, :ch] * jax.nn.sigmoid(u[:, ch:])
    pad_ref[0:front, :] = jnp.zeros((front, ch), F32)
    pad_ref[front:front + seq, :] = h
    tail = pad_ref.shape[0] - front - seq
    pad_ref[front + seq:, :] = jnp.zeros((tail, ch), F32)
    win_rows = tile + 40
    half = (CONV_WIDTH - 1) // 2

    def body(t, carry):
        base = pl.multiple_of(t * tile, tile)
        win = pad_ref[pl.ds(base, win_rows), :]
        acc = jnp.zeros((tile, ch), F32)
        for r in range(8):
            wr = win[r:r + tile + 32]
            for a in range(4):
                k = 8 * a + r - (front - half)
                if 0 <= k < CONV_WIDTH:
                    acc = acc + wr[8 * a:8 * a + tile] * w_ref[k:k + 1, :]
        acc = acc + b_ref[...]
        mu = jnp.mean(acc, axis=-1, keepdims=True)
        xc = acc - mu
        var = jnp.mean(xc * xc, axis=-1, keepdims=True)
        y = xc * lax.rsqrt(var + 1e-5) * lnw_ref[...] + lnb_ref[...]
        o_ref[pl.ds(base, tile), :] = _silu(y).astype(o_ref.dtype)
        return carry

    lax.fori_loop(0, seq // tile, body, 0)


def _convmod(uconv, dw_w, dw_b, ln_w, ln_b, layer):
    b, s, c2 = uconv.shape
    ch = c2 // 2
    tile = 64
    vec = lambda: pl.BlockSpec((None, 1, ch), lambda i: (layer, 0, 0))
    return pl.pallas_call(
        functools.partial(_convmod_kernel, seq=s, ch=ch, tile=tile),
        grid=(b,),
        in_specs=[
            pl.BlockSpec((None, s, c2), lambda i: (i, 0, 0)),
            pl.BlockSpec((None, CONV_WIDTH, ch), lambda i: (layer, 0, 0)),
            vec(), vec(), vec(),
        ],
        out_specs=pl.BlockSpec((None, s, ch), lambda i: (i, 0, 0)),
        out_shape=jax.ShapeDtypeStruct((b, s, ch), BF16),
        scratch_shapes=[pltpu.VMEM((s + 40, ch), F32)],
        compiler_params=pltpu.CompilerParams(dimension_semantics=("arbitrary",), vmem_limit_bytes=VMEM_LIMIT),
        name="convmod",
    )(uconv, dw_w, dw_b, ln_w, ln_b)


def _pos_features(pos, slope):
    return slope * (pos & -CHUNK).astype(F32), slope * (pos & (CHUNK - 1)).astype(F32)


def _diffattn_kernel(lam_ref, q_ref, k_ref, v_ref, sw_ref, o_ref, kx_s, vx_s, *, tq, seq, slopes, lambda_init):
    p = pl.program_id(1)
    qi = pl.program_id(2)
    n_kt = seq // tq
    hd, dd = HEAD_DIM, DIFF_DIM
    n_pairs = len(slopes) // 2
    assert all(math.log2(s).is_integer() for s in slopes)

    def head_slope(hh):
        slope = jnp.float32(slopes[2 * (n_pairs - 1) + hh])
        for pp in range(n_pairs - 2, -1, -1):
            slope = jnp.where(p == pp, jnp.float32(slopes[2 * pp + hh]), slope)
        return slope

    def feature_lanes(j, shape):
        lane = lax.broadcasted_iota(jnp.int32, shape, 1)
        data = (lane >= dd * j) & (lane < dd * (j + 1))
        return data, lane - dd * (1 - j)

    @pl.when(qi == 0)
    def _():
        pos = lax.broadcasted_iota(jnp.int32, (seq, hd), 0)
        fl = lax.broadcasted_iota(jnp.int32, (seq, hd), 1) & (dd - 1)
        for hh in range(2):
            f1, f2 = _pos_features(pos, head_slope(hh))
            kh = k_ref[:, hd * hh:hd * (hh + 1)].astype(F32)
            feat = jnp.where(fl < 2, 1.0, jnp.where(fl == 2, f1, jnp.where(fl == 3, f2, 0.0)))
            for j in range(2):
                data, _ = feature_lanes(j, (seq, hd))
                kx_s[0, 2 * hh + j] = jnp.where(data, kh, feat).astype(BF16)
                kx_s[1, 2 * hh + j] = jnp.where(data, kh, -feat).astype(BF16)
            vx_s[hh] = jnp.concatenate([v_ref[:, hd * hh:hd * (hh + 1)], jnp.ones((seq, hd), BF16)], axis=1)

    lp = lam_ref[...]
    lam = (jnp.exp(jnp.sum(lp[0:1] * lp[1:2], axis=-1, keepdims=True))
           - jnp.exp(jnp.sum(lp[2:3] * lp[3:4], axis=-1, keepdims=True)) + lambda_init)
    scale = dd ** -0.5
    qpos = lax.broadcasted_iota(jnp.int32, (tq, hd), 0) + qi * tq
    rr = lax.broadcasted_iota(jnp.int32, (tq, tq), 0)
    cc = lax.broadcasted_iota(jnp.int32, (tq, tq), 1)
    ahead = jnp.maximum(cc - rr, 0).astype(F32)
    tiles = []
    for rel in range(n_kt):
        kt = qi + rel
        kt = jnp.where(kt >= n_kt, kt - n_kt, kt)
        tiles.append((jnp.where(kt > qi, 1, 0), pl.ds(pl.multiple_of(kt * tq, tq), tq)))

    def scores(hh, j):
        slope = head_slope(hh)
        f1, f2 = _pos_features(qpos, slope)
        qh = q_ref[:, hd * hh:hd * (hh + 1)].astype(F32) * scale
        data, fl = feature_lanes(j, (tq, hd))
        feat = jnp.where(fl == 0, -f1, jnp.where(fl == 1, -f2, jnp.where(fl < 4, 1.0, 0.0)))
        qx = jnp.where(data, qh, feat).astype(BF16)
        s_tiles = [_mm_nt(qx, kx_s[side, 2 * hh + j, rows, :]) for side, rows in tiles]
        s_tiles[0] = s_tiles[0] + ahead * (-2.0 * slope)
        return jnp.concatenate(s_tiles, axis=1)

    maps = [(hh, j) for hh in range(2) for j in range(2)]
    s_next = scores(*maps[0])
    outs = []
    for n, (hh, j) in enumerate(maps):
        s = s_next
        if n + 1 < len(maps):
            s_next = scores(*maps[n + 1])
        m = jnp.max(s, axis=-1, keepdims=True)
        e = jnp.exp((s - m).astype(BF16))
        ov = None
        for r, (side, rows) in enumerate(tiles):
            part = _mm(e[:, r * tq:(r + 1) * tq], vx_s[hh, rows, :])
            ov = part if ov is None else ov + part
        outs.append(ov[:, :hd] / ov[:, hd:hd + 1])
        if j == 1:
            o = outs[-2] - lam * outs[-1]
            o = _rms(o, sw_ref[...], 1e-5) * (1.0 - lambda_init)
            o_ref[:, hd * hh:hd * (hh + 1)] = o.astype(o_ref.dtype)


def _diffattn(dqkv, lam_all, subln_w, layer, n_heads, tq):
    b, s, w3 = dqkv.shape
    n_pairs = n_heads // 2
    lambda_init = 0.8 - 0.6 * math.exp(-0.3 * layer)
    slopes = tuple(_alibi_slopes(n_heads))
    return pl.pallas_call(
        functools.partial(_diffattn_kernel, tq=tq, seq=s, slopes=slopes, lambda_init=lambda_init),
        grid=(b, n_pairs, s // tq),
        in_specs=[
            pl.BlockSpec((None, 4, DIFF_DIM), lambda i, p, q: (layer, 0, 0)),
            pl.BlockSpec((None, tq, LANES), lambda i, p, q: (i, q, p)),
            pl.BlockSpec((None, s, LANES), lambda i, p, q: (i, 0, n_pairs + p)),
            pl.BlockSpec((None, s, LANES), lambda i, p, q: (i, 0, 2 * n_pairs + p)),
            pl.BlockSpec((None, 1, HEAD_DIM), lambda i, p, q: (layer, 0, 0)),
        ],
        out_specs=pl.BlockSpec((None, tq, LANES), lambda i, p, q: (i, q, p)),
        out_shape=jax.ShapeDtypeStruct((b, s, n_pairs * LANES), BF16),
        scratch_shapes=[
            pltpu.VMEM((2, 4, s, HEAD_DIM), BF16),
            pltpu.VMEM((2, s, LANES), BF16),
        ],
        compiler_params=pltpu.CompilerParams(
            dimension_semantics=("arbitrary", "arbitrary", "arbitrary"), vmem_limit_bytes=VMEM_LIMIT),
        name="diffattn",
    )(lam_all, dqkv, dqkv, dqkv, subln_w)


def _unit_tri_inverses(mats, ii, jj):
    eye = (ii == jj).astype(F32)
    diag16 = (ii // 16) == (jj // 16)
    bds = [jnp.where(diag16, -a, 0.0) for a in mats]
    bdb = [x.astype(BF16) for x in bds]
    ps = [eye + x for x in bds]
    qs = [_mm(x, x) for x in bdb]
    for _ in range(2):
        pqs = [_mm(q.astype(BF16), jnp.concatenate([p, q], axis=1).astype(BF16)) for p, q in zip(ps, qs)]
        ps = [p + pq[:, :CHUNK] for p, pq in zip(ps, pqs)]
        qs = [pq[:, CHUNK:] for pq in pqs]
    ts = [p + _mm(q.astype(BF16), p.astype(BF16)) for p, q in zip(ps, qs)]
    for blk in (16, 32):
        join = ((ii // (2 * blk)) == (jj // (2 * blk))) & ((ii // blk) != (jj // blk))
        lbs = [jnp.where(join, a, 0.0).astype(BF16) for a in mats]
        tbs = [t.astype(BF16) for t in ts]
        tls = [_mm(tb, lb).astype(BF16) for tb, lb in zip(tbs, lbs)]
        ts = [t - _mm(tl, tb) for t, tl, tb in zip(ts, tls, tbs)]
    return ts


def _deltanet_kernel(gq_ref, gk_ref, gv_ref, z_ref, gate_ref, cwq_ref, cwk_ref, cwv_ref, gpar_ref, nw_ref,
                     o_ref,
                     pad_s, kbf_s, qbf_s, qg_s, kdec_s, rhs_s, gc_s, be_s, gl_s, cc_s, wp_s, op_s, qp_s, osum_s,
                     *, seq, n_gate_heads, group_chunks):
    p = pl.program_id(1)
    n_chunks = seq // CHUNK
    hd = HEAD_DIM

    il = lax.broadcasted_iota(jnp.int32, (LANES, LANES), 0)
    jl = lax.broadcasted_iota(jnp.int32, (LANES, LANES), 1)
    same_head = ((il // hd) == (jl // hd)).astype(BF16)

    edge = jnp.zeros((8, LANES), F32)
    pad_s[0:8, :] = edge
    pad_s[8 + seq:, :] = edge

    def short_conv(x_ref, w_ref):
        pad_s[8:8 + seq, :] = x_ref[...].astype(F32)
        y = (pad_s[7:7 + seq, :] * w_ref[0:1, :] + pad_s[8:8 + seq, :] * w_ref[1:2, :]
             + pad_s[9:9 + seq, :] * w_ref[2:3, :])
        return _silu(y)

    def l2n(x):
        return x * lax.rsqrt(_mm_exact_rhs(x * x, same_head) + 1e-6)

    q = l2n(short_conv(gq_ref, cwq_ref)) * (hd ** -0.5)
    k = l2n(short_conv(gk_ref, cwk_ref))
    v = short_conv(gv_ref, cwv_ref)
    kbf_s[...] = k.astype(BF16)
    qbf_s[...] = q.astype(BF16)

    nh = n_gate_heads
    raw = gate_ref[...].T[0:GATE_ROWS]
    reps = seq // LANES
    a_log = jnp.tile(gpar_ref[0], (1, reps))
    dt_bias = jnp.tile(gpar_ref[1], (1, reps))
    beta = jax.nn.sigmoid(raw)
    g = -jnp.exp(a_log) * _softplus(raw + dt_bias)
    gate_row = lax.broadcasted_iota(jnp.int32, (GATE_ROWS, seq), 0)
    pos_in_chunk = lax.broadcasted_iota(jnp.int32, (GATE_ROWS, seq), 1) & (CHUNK - 1)
    fwd, rev = g, g
    for sh in (1, 2, 4, 8, 16, 32):
        fwd = fwd + jnp.where(pos_in_chunk >= sh, pltpu.roll(fwd, sh, 1), 0.0)
        rev = rev + jnp.where(pos_in_chunk < CHUNK - sh, pltpu.roll(rev, seq - sh, 1), 0.0)
    gc_r = jnp.where(gate_row < 3 * nh, fwd, rev)

    ir = lax.broadcasted_iota(jnp.int32, (GATE_ROWS, 2 * LANES), 0)
    jr = lax.broadcasted_iota(jnp.int32, (GATE_ROWS, 2 * LANES), 1)

    def expand(x, base, pieces):
        sel = (ir == base + nh * (jr // LANES) + 2 * p + (jr % LANES) // hd).astype(BF16)
        out = None
        for _ in range(pieces):
            xb = x.astype(BF16)
            part = _mm_tn(xb, sel)
            out = part if out is None else out + part
            x = x - xb.astype(F32)
        return out

    be_all = expand(beta, 0, 2)
    gc_all = expand(gc_r, 2 * nh, 3)

    for d in range(2):
        dl = slice(LANES * d, LANES * (d + 1))
        be = be_all[:, dl]
        gc = gc_all[:, dl]
        gc3 = gc.reshape(n_chunks, CHUNK, LANES)
        last = CHUNK - 1 if d == 0 else 0
        total3 = jnp.broadcast_to(gc3[:, last:last + 1, :], gc3.shape)
        eg = jnp.exp(gc)
        kb = k * be
        vb = (v * be).astype(BF16)
        kbe = (kb * eg).astype(BF16)
        gc_s[d] = gc
        be_s[d] = be
        gl_s[d] = jnp.exp(total3).reshape(seq, LANES)
        qg_s[d] = (q * eg).astype(BF16)
        kdec_s[d] = (k * jnp.exp(total3 - gc3).reshape(seq, LANES)).astype(BF16)
        rhs_s[d, :, 0 * hd:1 * hd] = vb[:, :hd]
        rhs_s[d, :, 1 * hd:2 * hd] = kbe[:, :hd]
        rhs_s[d, :, 2 * hd:3 * hd] = vb[:, hd:]
        rhs_s[d, :, 3 * hd:4 * hd] = kbe[:, hd:]

    ii = lax.broadcasted_iota(jnp.int32, (CHUNK, CHUNK), 0)
    jj = lax.broadcasted_iota(jnp.int32, (CHUNK, CHUNK), 1)

    def prep(g, carry):
        keys, a_list, aqk_list = [], [], []
        for ci in range(group_chunks):
            c = g * group_chunks + ci
            rows = pl.ds(pl.multiple_of(c * CHUNK, CHUNK), CHUNK)
            for hh in range(2):
                ls = slice(hd * hh, hd * (hh + 1))
                kh = kbf_s[rows, ls]
                kq = _mm_nt(jnp.concatenate([kh, qbf_s[rows, ls]], axis=0), kh)
                kk, qk = kq[:CHUNK], kq[CHUNK:]
                for d in range(2):
                    x = gc_s[d, rows, ls]
                    diff = x - x.T
                    incl = (ii >= jj) if d == 0 else (ii <= jj)
                    strict = (ii > jj) if d == 0 else (ii < jj)
                    dec = jnp.exp(jnp.where(incl, diff, -1e30))
                    a_list.append(jnp.where(strict, kk * dec, 0.0) * be_s[d, rows, ls])
                    aqk_list.append((qk * dec).astype(BF16))
                    keys.append((c, rows, hh, d, ls))
        ts = _unit_tri_inverses(a_list, ii, jj)
        uws = [_mm(t.astype(BF16), rhs_s[d, rows, 2 * hd * hh:2 * hd * (hh + 1)]).astype(BF16)
               for t, (c, rows, hh, d, ls) in zip(ts, keys)]
        kds = [_mm_tn(kdec_s[d, rows, ls], uw) for uw, (c, rows, hh, d, ls) in zip(uws, keys)]
        aos = [_mm(aqk, uw) for aqk, uw in zip(aqk_list, uws)]
        for kd, ao, (c, rows, hh, d, ls) in zip(kds, aos, keys):
            idx = 2 * d + hh
            cc_s[idx, c] = kd[:, :hd]
            wp_s[idx, c] = kd[:, hd:].astype(BF16)
            op_s[idx, c] = ao[:, :hd]
            qp_s[idx, c] = (qg_s[d, rows, ls].astype(F32) - ao[:, hd:]).astype(BF16)
        return carry

    lax.fori_loop(0, n_chunks // group_chunks, prep, 0)

    def scan(n, states):
        new = []
        for d in range(2):
            c = n if d == 0 else n_chunks - 1 - n
            rows = pl.ds(pl.multiple_of(c * CHUNK, CHUNK), CHUNK)
            for hh in range(2):
                idx = 2 * d + hh
                ls = slice(hd * hh, hd * (hh + 1))
                st = states[idx]
                sb = st.astype(BF16)
                osum_s[d, rows, ls] = _mm(qp_s[idx, c], sb) + op_s[idx, c]
                new.append(st * gl_s[d, rows, ls] - _mm(wp_s[idx, c], sb) + cc_s[idx, c])
        return tuple(new)

    zero = jnp.zeros((hd, hd), F32)
    lax.fori_loop(0, n_chunks, scan, (zero,) * 4)

    o = osum_s[0] + osum_s[1]
    ms = _mm_exact_rhs(o * o, same_head) * (1.0 / hd)
    y = o * lax.rsqrt(ms + NORM_EPS) * nw_ref[...] * _silu(z_ref[...].astype(F32))
    o_ref[...] = y.astype(o_ref.dtype)


def _deltanet(gqkv, gz, gates, conv_w, gpar, norm_w2, layer, n_heads):
    b, s, w3 = gqkv.shape
    n_pairs = n_heads // 2
    n_chunks = s // CHUNK
    blk = lambda off: pl.BlockSpec((None, s, LANES), lambda i, p: (i, 0, off + p))
    cw = lambda off: pl.BlockSpec((None, SHORT_CONV, LANES), lambda i, p: (layer, 0, off + p))
    return pl.pallas_call(
        functools.partial(_deltanet_kernel, seq=s, n_gate_heads=n_heads, group_chunks=min(4, n_chunks)),
        grid=(b, n_pairs),
        in_specs=[
            blk(0), blk(n_pairs), blk(2 * n_pairs),
            pl.BlockSpec((None, s, LANES), lambda i, p: (i, 0, p)),
            pl.BlockSpec((None, s, LANES), lambda i, p: (i, 0, 0)),
            cw(0), cw(n_pairs), cw(2 * n_pairs),
            pl.BlockSpec((None, 2, GATE_ROWS, LANES), lambda i, p: (layer, 0, 0, 0)),
            pl.BlockSpec((None, 1, LANES), lambda i, p: (layer, 0, 0)),
        ],
        out_specs=pl.BlockSpec((None, s, LANES), lambda i, p: (i, 0, p)),
        out_shape=jax.ShapeDtypeStruct((b, s, n_pairs * LANES), BF16),
        scratch_shapes=[
            pltpu.VMEM((s + 16, LANES), F32),
            pltpu.VMEM((s, LANES), BF16),
            pltpu.VMEM((s, LANES), BF16),
            pltpu.VMEM((2, s, LANES), BF16),
            pltpu.VMEM((2, s, LANES), BF16),
            pltpu.VMEM((2, s, 2 * LANES), BF16),
            pltpu.VMEM((2, s, LANES), F32),
            pltpu.VMEM((2, s, LANES), F32),
            pltpu.VMEM((2, s, LANES), F32),
            pltpu.VMEM((4, n_chunks, HEAD_DIM, HEAD_DIM), F32),
            pltpu.VMEM((4, n_chunks, HEAD_DIM, HEAD_DIM), BF16),
            pltpu.VMEM((4, n_chunks, CHUNK, HEAD_DIM), F32),
            pltpu.VMEM((4, n_chunks, CHUNK, HEAD_DIM), BF16),
            pltpu.VMEM((2, s, LANES), F32),
        ],
        compiler_params=pltpu.CompilerParams(
            dimension_semantics=("arbitrary", "arbitrary"), vmem_limit_bytes=VMEM_LIMIT),
        name="deltanet",
    )(gqkv, gqkv, gqkv, gz, gates, conv_w, conv_w, conv_w, gpar, norm_w2)


def _out_proj_kernel(yc_ref, yd_ref, yg_ref, w_ref, x_ref, nw_ref, o_ref):
    c0 = yc_ref.shape[-1]
    c1 = c0 + yd_ref.shape[-1]
    y = (_mm(yc_ref[...], w_ref[0:c0, :]) + _mm(yd_ref[...], w_ref[c0:c1, :])
         + _mm(yg_ref[...], w_ref[c1:, :]))
    o_ref[...] = x_ref[...] + _rms(y, nw_ref[...], NORM_EPS)


def _out_proj(yc, yd, yg, w_all, x2, nw, layer, tm):
    m, d = x2.shape
    row = lambda a: pl.BlockSpec((tm, a.shape[-1]), lambda i: (i, 0))
    return pl.pallas_call(
        _out_proj_kernel,
        grid=(m // tm,),
        in_specs=[
            row(yc), row(yd), row(yg),
            pl.BlockSpec((None, d, d), lambda i: (layer, 0, 0)),
            pl.BlockSpec((tm, d), lambda i: (i, 0)),
            pl.BlockSpec((None, 1, d), lambda i: (layer, 0, 0)),
        ],
        out_specs=pl.BlockSpec((tm, d), lambda i: (i, 0)),
        out_shape=jax.ShapeDtypeStruct((m, d), F32),
        compiler_params=pltpu.CompilerParams(dimension_semantics=("arbitrary",), vmem_limit_bytes=VMEM_LIMIT),
        name="out_proj",
    )(yc, yd, yg, w_all, x2, nw)


def _ffn_kernel(x_ref, nw1_ref, w1_ref, w2_ref, nw2_ref, o_ref, hb_s, acc_s):
    j = pl.program_id(1)

    @pl.when(j == 0)
    def _():
        hb_s[...] = _rms(x_ref[...], nw1_ref[...], NORM_EPS).astype(BF16)

    a = _mm(hb_s[...], w1_ref[...])
    a = jnp.square(jnp.maximum(a, 0.0)).astype(BF16)
    part = _mm(a, w2_ref[...])

    @pl.when(j == 0)
    def _():
        acc_s[...] = part

    @pl.when(j > 0)
    def _():
        acc_s[...] = acc_s[...] + part

    @pl.when(j == pl.num_programs(1) - 1)
    def _():
        o_ref[...] = x_ref[...] + _rms(acc_s[...], nw2_ref[...], NORM_EPS)


def _ffn(x2, nw1, w1_all, w2_all, nw2, layer, tm, tf):
    m, d = x2.shape
    dff = w1_all.shape[-1]
    return pl.pallas_call(
        _ffn_kernel,
        grid=(m // tm, dff // tf),
        in_specs=[
            pl.BlockSpec((tm, d), lambda i, j: (i, 0)),
            pl.BlockSpec((None, 1, d), lambda i, j: (layer, 0, 0)),
            pl.BlockSpec((None, d, tf), lambda i, j: (layer, 0, j)),
            pl.BlockSpec((None, tf, d), lambda i, j: (layer, j, 0)),
            pl.BlockSpec((None, 1, d), lambda i, j: (layer, 0, 0)),
        ],
        out_specs=pl.BlockSpec((tm, d), lambda i, j: (i, 0)),
        out_shape=jax.ShapeDtypeStruct((m, d), F32),
        scratch_shapes=[pltpu.VMEM((tm, d), BF16), pltpu.VMEM((tm, d), F32)],
        compiler_params=pltpu.CompilerParams(
            dimension_semantics=("arbitrary", "arbitrary"), vmem_limit_bytes=VMEM_LIMIT),
        name="ffn",
    )(x2, nw1, w1_all, w2_all, nw2)


def kernel(x, w_in, w_out, pre_mix_w, post_mix_w, pre_mlp_w, post_mlp_w, w_ff1, w_ff2, conv_dw_w, conv_dw_b, conv_ln_w, conv_ln_b, diff_lambda_q1, diff_lambda_k1, diff_lambda_q2, diff_lambda_k2, diff_subln_w, delta_conv_w, delta_A_log, delta_dt_bias, delta_norm_w):
    b, s, d = x.shape
    depth = w_in.shape[0]
    conv_ch = conv_dw_w.shape[-1]
    delta_w = delta_conv_w.shape[-1] // 3
    n_delta_heads = delta_w // HEAD_DIM
    in_w = w_in.shape[-1]
    diff_w = (in_w - 2 * conv_ch - 4 * delta_w - 4 * n_delta_heads) // 3
    n_diff_heads = diff_w // HEAD_DIM
    n_gates = 4 * n_delta_heads
    splits = (2 * conv_ch, 3 * diff_w, 3 * delta_w, delta_w, LANES)
    m = b * s
    tm = min(512, m)

    w_in_b = jnp.pad(w_in, ((0, 0), (0, 0), (0, LANES - n_gates))).astype(BF16)
    w_out_b = w_out.astype(BF16)
    w1_b = w_ff1.astype(BF16)
    w2_b = w_ff2.astype(BF16)
    row3 = lambda a: a.reshape(depth, 1, a.shape[-1])
    lam_all = jnp.stack([diff_lambda_q1, diff_lambda_k1, diff_lambda_q2, diff_lambda_k2], axis=1)
    gate_par = jnp.stack([delta_A_log.reshape(depth, -1), delta_dt_bias.reshape(depth, -1)], axis=1)
    gate_par = jnp.pad(gate_par, ((0, 0), (0, 0), (2 * n_delta_heads, GATE_ROWS - n_gates)))
    gate_par = jnp.broadcast_to(gate_par[..., None], (depth, 2, GATE_ROWS, LANES))
    delta_nw2 = row3(jnp.concatenate([delta_norm_w, delta_norm_w], axis=-1))

    x2 = x.reshape(m, d)
    for l in range(depth):
        uconv, dqkv, gqkv, gz, gates = _in_proj(x2, row3(pre_mix_w), w_in_b, l, splits, tm)
        y_conv = _convmod(uconv.reshape(b, s, -1), conv_dw_w, row3(conv_dw_b), row3(conv_ln_w),
                          row3(conv_ln_b), l)
        y_diff = _diffattn(dqkv.reshape(b, s, -1), lam_all, row3(diff_subln_w), l, n_diff_heads, min(256, s))
        y_delta = _deltanet(gqkv.reshape(b, s, -1), gz.reshape(b, s, -1), gates.reshape(b, s, -1),
                            delta_conv_w, gate_par, delta_nw2, l, n_delta_heads)
        x2 = _out_proj(y_conv.reshape(m, -1), y_diff.reshape(m, -1), y_delta.reshape(m, -1),
                       w_out_b, x2, row3(post_mix_w), l, tm)
        x2 = _ffn(x2, row3(pre_mlp_w), w1_b, w2_b, row3(post_mlp_w), l, min(1024, m), 512)
    return x2.reshape(b, s, d)
```

```python
import functools
import math

import jax
import jax.numpy as jnp
from jax import lax
from jax.experimental import pallas as pl
from jax.experimental.pallas import tpu as pltpu

F32 = jnp.float32
BF16 = jnp.bfloat16

HEAD_DIM = 64
DIFF_DIM = 32
CONV_WIDTH = 31
SHORT_CONV = 3
CHUNK = 64
NORM_EPS = 1e-6
LANES = 128
GATE_ROWS = 32
VMEM_LIMIT = 56 * 1024 * 1024


def _alibi_slopes(n):
    def pow2(m):
        start = 2.0 ** (-8.0 / m)
        return [start ** (i + 1) for i in range(m)]
    if math.log2(n).is_integer():
        return pow2(n)
    c = 2 ** int(math.floor(math.log2(n)))
    return pow2(c) + pow2(2 * c)[0::2][: n - c]


def _rms(x, w, eps):
    return x * lax.rsqrt(jnp.mean(x * x, axis=-1, keepdims=True) + eps) * w


def _mm(a, b):
    return jnp.dot(a, b, preferred_element_type=F32)


def _mm_nt(a, b):
    return lax.dot_general(a, b, (((1,), (1,)), ((), ())), preferred_element_type=F32)


def _mm_tn(a, b):
    return lax.dot_general(a, b, (((0,), (0,)), ((), ())), preferred_element_type=F32)


def _mm_exact_rhs(x, w):
    hi = x.astype(BF16)
    r1 = x - hi.astype(F32)
    mid = r1.astype(BF16)
    lo = (r1 - mid.astype(F32)).astype(BF16)
    return _mm(hi, w) + _mm(mid, w) + _mm(lo, w)


def _softplus(x):
    return jnp.maximum(x, 0.0) + jnp.log1p(jnp.exp(-jnp.abs(x)))


def _silu(x):
    h = 0.5 * x
    return h + h * jnp.tanh(h)


def _in_proj_kernel(x_ref, nw_ref, w_ref, uconv_ref, dqkv_ref, gqkv_ref, gz_ref, gate_ref, *, splits):
    hb = _rms(x_ref[...], nw_ref[...], NORM_EPS).astype(BF16)
    outs = (uconv_ref, dqkv_ref, gqkv_ref, gz_ref, gate_ref)
    off = 0
    for o_ref, width in zip(outs, splits):
        o_ref[...] = _mm(hb, w_ref[:, off:off + width]).astype(o_ref.dtype)
        off += width


def _in_proj(x2, nw, w_all, layer, splits, tm):
    m, d = x2.shape
    npad = w_all.shape[-1]
    dts = (BF16, BF16, BF16, BF16, F32)
    return pl.pallas_call(
        functools.partial(_in_proj_kernel, splits=splits),
        grid=(m // tm,),
        in_specs=[
            pl.BlockSpec((tm, d), lambda i: (i, 0)),
            pl.BlockSpec((None, 1, d), lambda i: (layer, 0, 0)),
            pl.BlockSpec((None, d, npad), lambda i: (layer, 0, 0)),
        ],
        out_specs=[pl.BlockSpec((tm, w), lambda i: (i, 0)) for w in splits],
        out_shape=[jax.ShapeDtypeStruct((m, w), dt) for w, dt in zip(splits, dts)],
        compiler_params=pltpu.CompilerParams(dimension_semantics=("arbitrary",), vmem_limit_bytes=VMEM_LIMIT),
        name="in_proj",
    )(x2, nw, w_all)


def _convmod_kernel(u_ref, w_ref, b_ref, lnw_ref, lnb_ref, o_ref, pad_ref, *, seq, ch, tile):
    front = 16
    u = u_ref[...].astype(F32)
    hg = 0.5 * u[:, ch:]
    h = u[:, :ch] * (0.5 + 0.5 * jnp.tanh(hg))
    n_cols = ch // LANES
    tail = pad_ref.shape[1] - front - seq
    for c in range(n_cols):
        pad_ref[c, 0:front, :] = jnp.zeros((front, LANES), F32)
        pad_ref[c, front:front + seq, :] = h[:, c * LANES:(c + 1) * LANES]
        pad_ref[c, front + seq:, :] = jnp.zeros((tail, LANES), F32)
    half = (CONV_WIDTH - 1) // 2

    for t in range(seq // tile):
        base = t * tile
        cols = []
        for c in range(n_cols):
            part = jnp.zeros((tile, LANES), F32)
            for k in range(CONV_WIDTH):
                start = base + k + front - half
                part = part + pad_ref[c, start:start + tile, :] * w_ref[k:k + 1, c * LANES:(c + 1) * LANES]
            cols.append(part)
        acc = jnp.concatenate(cols, axis=1) + b_ref[...]
        mu = jnp.mean(acc, axis=-1, keepdims=True)
        xc = acc - mu
        var = jnp.mean(xc * xc, axis=-1, keepdims=True)
        y = xc * lax.rsqrt(var + 1e-5) * lnw_ref[...] + lnb_ref[...]
        o_ref[base:base + tile, :] = _silu(y).astype(o_ref.dtype)


def _convmod(uconv, dw_w, dw_b, ln_w, ln_b, layer):
    b, s, c2 = uconv.shape
    ch = c2 // 2
    tile = 64
    vec = lambda: pl.BlockSpec((None, 1, ch), lambda i: (layer, 0, 0))
    return pl.pallas_call(
        functools.partial(_convmod_kernel, seq=s, ch=ch, tile=tile),
        grid=(b,),
        in_specs=[
            pl.BlockSpec((None, s, c2), lambda i: (i, 0, 0)),
            pl.BlockSpec((None, CONV_WIDTH, ch), lambda i: (layer, 0, 0)),
            vec(), vec(), vec(),
        ],
        out_specs=pl.BlockSpec((None, s, ch), lambda i: (i, 0, 0)),
        out_shape=jax.ShapeDtypeStruct((b, s, ch), BF16),
        scratch_shapes=[pltpu.VMEM((ch // LANES, s + 40, LANES), F32)],
        compiler_params=pltpu.CompilerParams(dimension_semantics=("arbitrary",), vmem_limit_bytes=VMEM_LIMIT),
        name="convmod",
    )(uconv, dw_w, dw_b, ln_w, ln_b)


def _pos_features(pos, slope):
    return slope * (pos & -CHUNK).astype(F32), slope * (pos & (CHUNK - 1)).astype(F32)


def _diffattn_kernel(lam_ref, q_ref, k_ref, v_ref, sw_ref, o_ref, kx_s, vx_s, *, tq, seq, slopes, lambda_init):
    p = pl.program_id(1)
    qi = pl.program_id(2)
    n_kt = seq // tq
    hd, dd = HEAD_DIM, DIFF_DIM
    n_pairs = len(slopes) // 2
    assert all(math.log2(s).is_integer() for s in slopes)

    def head_slope(hh):
        slope = jnp.float32(slopes[2 * (n_pairs - 1) + hh])
        for pp in range(n_pairs - 2, -1, -1):
            slope = jnp.where(p == pp, jnp.float32(slopes[2 * pp + hh]), slope)
        return slope

    def feature_lanes(j, shape):
        lane = lax.broadcasted_iota(jnp.int32, shape, 1)
        data = (lane >= dd * j) & (lane < dd * (j + 1))
        return data, lane - dd * (1 - j)

    @pl.when(qi == 0)
    def _():
        pos = lax.broadcasted_iota(jnp.int32, (seq, hd), 0)
        fl = lax.broadcasted_iota(jnp.int32, (seq, hd), 1) & (dd - 1)
        for hh in range(2):
            f1, f2 = _pos_features(pos, head_slope(hh))
            kh = k_ref[:, hd * hh:hd * (hh + 1)].astype(F32)
            feat = jnp.where(fl < 2, 1.0, jnp.where(fl == 2, f1, jnp.where(fl == 3, f2, 0.0)))
            for j in range(2):
                data, _ = feature_lanes(j, (seq, hd))
                kx_s[0, 2 * hh + j] = jnp.where(data, kh, feat).astype(BF16)
                kx_s[1, 2 * hh + j] = jnp.where(data, kh, -feat).astype(BF16)
            vx_s[hh] = jnp.concatenate([v_ref[:, hd * hh:hd * (hh + 1)], jnp.ones((seq, hd), BF16)], axis=1)

    lp = lam_ref[...]
    lam = (jnp.exp(jnp.sum(lp[0:1] * lp[1:2], axis=-1, keepdims=True))
           - jnp.exp(jnp.sum(lp[2:3] * lp[3:4], axis=-1, keepdims=True)) + lambda_init)
    scale = dd ** -0.5
    qpos = lax.broadcasted_iota(jnp.int32, (tq, hd), 0) + qi * tq
    rr = lax.broadcasted_iota(jnp.int32, (tq, tq), 0)
    cc = lax.broadcasted_iota(jnp.int32, (tq, tq), 1)
    ahead = jnp.maximum(cc - rr, 0).astype(F32)
    tiles = []
    for rel in range(n_kt):
        kt = qi + rel
        kt = jnp.where(kt >= n_kt, kt - n_kt, kt)
        tiles.append((jnp.where(kt > qi, 1, 0), pl.ds(pl.multiple_of(kt * tq, tq), tq)))

    def scores(hh, j):
        slope = head_slope(hh)
        f1, f2 = _pos_features(qpos, slope)
        qh = q_ref[:, hd * hh:hd * (hh + 1)].astype(F32) * scale
        data, fl = feature_lanes(j, (tq, hd))
        feat = jnp.where(fl == 0, -f1, jnp.where(fl == 1, -f2, jnp.where(fl < 4, 1.0, 0.0)))
        qx = jnp.where(data, qh, feat).astype(BF16)
        s_tiles = [_mm_nt(qx, kx_s[side, 2 * hh + j, rows, :]) for side, rows in tiles]
        s_tiles[0] = s_tiles[0] + ahead * (-2.0 * slope)
        return jnp.concatenate(s_tiles, axis=1)

    maps = [(hh, j) for hh in range(2) for j in range(2)]
    s_next = scores(*maps[0])
    outs = []
    for n, (hh, j) in enumerate(maps):
        s = s_next
        if n + 1 < len(maps):
            s_next = scores(*maps[n + 1])
        m = jnp.max(s, axis=-1, keepdims=True)
        e = jnp.exp((s - m).astype(BF16))
        ov = None
        for r, (side, rows) in enumerate(tiles):
            part = _mm(e[:, r * tq:(r + 1) * tq], vx_s[hh, rows, :])
            ov = part if ov is None else ov + part
        outs.append(ov[:, :hd] / ov[:, hd:hd + 1])
        if j == 1:
            o = outs[-2] - lam * outs[-1]
            o = _rms(o, sw_ref[...], 1e-5) * (1.0 - lambda_init)
            o_ref[:, hd * hh:hd * (hh + 1)] = o.astype(o_ref.dtype)


def _diffattn(dqkv, lam_all, subln_w, layer, n_heads, tq):
    b, s, w3 = dqkv.shape
    n_pairs = n_heads // 2
    lambda_init = 0.8 - 0.6 * math.exp(-0.3 * layer)
    slopes = tuple(_alibi_slopes(n_heads))
    return pl.pallas_call(
        functools.partial(_diffattn_kernel, tq=tq, seq=s, slopes=slopes, lambda_init=lambda_init),
        grid=(b, n_pairs, s // tq),
        in_specs=[
            pl.BlockSpec((None, 4, DIFF_DIM), lambda i, p, q: (layer, 0, 0)),
            pl.BlockSpec((None, tq, LANES), lambda i, p, q: (i, q, p)),
            pl.BlockSpec((None, s, LANES), lambda i, p, q: (i, 0, n_pairs + p)),
            pl.BlockSpec((None, s, LANES), lambda i, p, q: (i, 0, 2 * n_pairs + p)),
            pl.BlockSpec((None, 1, HEAD_DIM), lambda i, p, q: (layer, 0, 0)),
        ],
        out_specs=pl.BlockSpec((None, tq, LANES), lambda i, p, q: (i, q, p)),
        out_shape=jax.ShapeDtypeStruct((b, s, n_pairs * LANES), BF16),
        scratch_shapes=[
            pltpu.VMEM((2, 4, s, HEAD_DIM), BF16),
            pltpu.VMEM((2, s, LANES), BF16),
        ],
        compiler_params=pltpu.CompilerParams(
            dimension_semantics=("arbitrary", "arbitrary", "arbitrary"), vmem_limit_bytes=VMEM_LIMIT),
        name="diffattn",
    )(lam_all, dqkv, dqkv, dqkv, subln_w)


def _unit_tri_inverses(mats, ii, jj):
    eye = (ii == jj).astype(F32)
    diag16 = (ii // 16) == (jj // 16)
    bds = [jnp.where(diag16, -a, 0.0) for a in mats]
    bdb = [x.astype(BF16) for x in bds]
    ps = [eye + x for x in bds]
    qs = [_mm(x, x) for x in bdb]
    for _ in range(2):
        pqs = [_mm(q.astype(BF16), jnp.concatenate([p, q], axis=1).astype(BF16)) for p, q in zip(ps, qs)]
        ps = [p + pq[:, :CHUNK] for p, pq in zip(ps, pqs)]
        qs = [pq[:, CHUNK:] for pq in pqs]
    ts = [p + _mm(q.astype(BF16), p.astype(BF16)) for p, q in zip(ps, qs)]
    for blk in (16, 32):
        join = ((ii // (2 * blk)) == (jj // (2 * blk))) & ((ii // blk) != (jj // blk))
        lbs = [jnp.where(join, a, 0.0).astype(BF16) for a in mats]
        tbs = [t.astype(BF16) for t in ts]
        tls = [_mm(tb, lb).astype(BF16) for tb, lb in zip(tbs, lbs)]
        ts = [t - _mm(tl, tb) for t, tl, tb in zip(ts, tls, tbs)]
    return ts


def _deltanet_kernel(gq_ref, gk_ref, gv_ref, z_ref, gate_ref, cwq_ref, cwk_ref, cwv_ref, gpar_ref, nw_ref,
                     o_ref,
                     pad_s, kbf_s, qbf_s, qg_s, kdec_s, rhs_s, gc_s, be_s, gl_s, cc_s, wp_s, op_s, qp_s, osum_s,
                     *, seq, n_gate_heads, group_chunks):
    p = pl.program_id(1)
    n_chunks = seq // CHUNK
    hd = HEAD_DIM

    il = lax.broadcasted_iota(jnp.int32, (LANES, LANES), 0)
    jl = lax.broadcasted_iota(jnp.int32, (LANES, LANES), 1)
    same_head = ((il // hd) == (jl // hd)).astype(BF16)

    edge = jnp.zeros((8, LANES), F32)
    pad_s[0:8, :] = edge
    pad_s[8 + seq:, :] = edge

    def short_conv(x_ref, w_ref):
        pad_s[8:8 + seq, :] = x_ref[...].astype(F32)
        y = (pad_s[7:7 + seq, :] * w_ref[0:1, :] + pad_s[8:8 + seq, :] * w_ref[1:2, :]
             + pad_s[9:9 + seq, :] * w_ref[2:3, :])
        return _silu(y)

    def l2n(x):
        return x * lax.rsqrt(_mm_exact_rhs(x * x, same_head) + 1e-6)

    q = l2n(short_conv(gq_ref, cwq_ref)) * (hd ** -0.5)
    k = l2n(short_conv(gk_ref, cwk_ref))
    v = short_conv(gv_ref, cwv_ref)
    kbf_s[...] = k.astype(BF16)
    qbf_s[...] = q.astype(BF16)

    nh = n_gate_heads
    raw = gate_ref[...].T[0:GATE_ROWS]
    reps = seq // LANES
    a_log = jnp.tile(gpar_ref[0], (1, reps))
    dt_bias = jnp.tile(gpar_ref[1], (1, reps))
    beta = jax.nn.sigmoid(raw)
    g = -jnp.exp(a_log) * _softplus(raw + dt_bias)
    gate_row = lax.broadcasted_iota(jnp.int32, (GATE_ROWS, seq), 0)
    pos_in_chunk = lax.broadcasted_iota(jnp.int32, (GATE_ROWS, seq), 1) & (CHUNK - 1)
    fwd, rev = g, g
    for sh in (1, 2, 4, 8, 16, 32):
        fwd = fwd + jnp.where(pos_in_chunk >= sh, pltpu.roll(fwd, sh, 1), 0.0)
        rev = rev + jnp.where(pos_in_chunk < CHUNK - sh, pltpu.roll(rev, seq - sh, 1), 0.0)
    gc_r = jnp.where(gate_row < 3 * nh, fwd, rev)

    ir = lax.broadcasted_iota(jnp.int32, (GATE_ROWS, 2 * LANES), 0)
    jr = lax.broadcasted_iota(jnp.int32, (GATE_ROWS, 2 * LANES), 1)

    def expand(x, base, pieces):
        sel = (ir == base + nh * (jr // LANES) + 2 * p + (jr % LANES) // hd).astype(BF16)
        out = None
        for _ in range(pieces):
            xb = x.astype(BF16)
            part = _mm_tn(xb, sel)
            out = part if out is None else out + part
            x = x - xb.astype(F32)
        return out

    be_all = expand(beta, 0, 2)
    gc_all = expand(gc_r, 2 * nh, 3)

    for d in range(2):
        dl = slice(LANES * d, LANES * (d + 1))
        be = be_all[:, dl]
        gc = gc_all[:, dl]
        gc3 = gc.reshape(n_chunks, CHUNK, LANES)
        last = CHUNK - 1 if d == 0 else 0
        total3 = jnp.broadcast_to(gc3[:, last:last + 1, :], gc3.shape)
        eg = jnp.exp(gc)
        kb = k * be
        vb = (v * be).astype(BF16)
        kbe = (kb * eg).astype(BF16)
        gc_s[d] = gc
        be_s[d] = be
        gl_s[d] = jnp.exp(total3).reshape(seq, LANES)
        qg_s[d] = (q * eg).astype(BF16)
        kdec_s[d] = (k * jnp.exp(total3 - gc3).reshape(seq, LANES)).astype(BF16)
        rhs_s[d, :, 0 * hd:1 * hd] = vb[:, :hd]
        rhs_s[d, :, 1 * hd:2 * hd] = kbe[:, :hd]
        rhs_s[d, :, 2 * hd:3 * hd] = vb[:, hd:]
        rhs_s[d, :, 3 * hd:4 * hd] = kbe[:, hd:]

    ii = lax.broadcasted_iota(jnp.int32, (CHUNK, CHUNK), 0)
    jj = lax.broadcasted_iota(jnp.int32, (CHUNK, CHUNK), 1)

    def prep(g, carry):
        keys, a_list, aqk_list = [], [], []
        for ci in range(group_chunks):
            c = g * group_chunks + ci
            rows = pl.ds(pl.multiple_of(c * CHUNK, CHUNK), CHUNK)
            for hh in range(2):
                ls = slice(hd * hh, hd * (hh + 1))
                kh = kbf_s[rows, ls]
                kq = _mm_nt(jnp.concatenate([kh, qbf_s[rows, ls]], axis=0), kh)
                kk, qk = kq[:CHUNK], kq[CHUNK:]
                for d in range(2):
                    x = gc_s[d, rows, ls]
                    diff = x - x.T
                    incl = (ii >= jj) if d == 0 else (ii <= jj)
                    strict = (ii > jj) if d == 0 else (ii < jj)
                    dec = jnp.exp(jnp.where(incl, diff, -1e30))
                    a_list.append(jnp.where(strict, kk * dec, 0.0) * be_s[d, rows, ls])
                    aqk_list.append((qk * dec).astype(BF16))
                    keys.append((c, rows, hh, d, ls))
        ts = _unit_tri_inverses(a_list, ii, jj)
        uws = [_mm(t.astype(BF16), rhs_s[d, rows, 2 * hd * hh:2 * hd * (hh + 1)]).astype(BF16)
               for t, (c, rows, hh, d, ls) in zip(ts, keys)]
        kds = [_mm_tn(kdec_s[d, rows, ls], uw) for uw, (c, rows, hh, d, ls) in zip(uws, keys)]
        aos = [_mm(aqk, uw) for aqk, uw in zip(aqk_list, uws)]
        for kd, ao, (c, rows, hh, d, ls) in zip(kds, aos, keys):
            idx = 2 * d + hh
            cc_s[idx, c] = kd[:, :hd]
            wp_s[idx, c] = kd[:, hd:].astype(BF16)
            op_s[idx, c] = ao[:, :hd]
            qp_s[idx, c] = (qg_s[d, rows, ls].astype(F32) - ao[:, hd:]).astype(BF16)
        return carry

    lax.fori_loop(0, n_chunks // group_chunks, prep, 0)

    def scan(n, states):
        new = []
        for d in range(2):
            c = n if d == 0 else n_chunks - 1 - n
            rows = pl.ds(pl.multiple_of(c * CHUNK, CHUNK), CHUNK)
            for hh in range(2):
                idx = 2 * d + hh
                ls = slice(hd * hh, hd * (hh + 1))
                st = states[idx]
                sb = st.astype(BF16)
                osum_s[d, rows, ls] = _mm(qp_s[idx, c], sb) + op_s[idx, c]
                new.append(st * gl_s[d, rows, ls] - _mm(wp_s[idx, c], sb) + cc_s[idx, c])
        return tuple(new)

    zero = jnp.zeros((hd, hd), F32)
    lax.fori_loop(0, n_chunks, scan, (zero,) * 4)

    o = osum_s[0] + osum_s[1]
    ms = _mm_exact_rhs(o * o, same_head) * (1.0 / hd)
    y = o * lax.rsqrt(ms + NORM_EPS) * nw_ref[...] * _silu(z_ref[...].astype(F32))
    o_ref[...] = y.astype(o_ref.dtype)


def _deltanet(gqkv, gz, gates, conv_w, gpar, norm_w2, layer, n_heads):
    b, s, w3 = gqkv.shape
    n_pairs = n_heads // 2
    n_chunks = s // CHUNK
    blk = lambda off: pl.BlockSpec((None, s, LANES), lambda i, p: (i, 0, off + p))
    cw = lambda off: pl.BlockSpec((None, SHORT_CONV, LANES), lambda i, p: (layer, 0, off + p))
    return pl.pallas_call(
        functools.partial(_deltanet_kernel, seq=s, n_gate_heads=n_heads, group_chunks=min(4, n_chunks)),
        grid=(b, n_pairs),
        in_specs=[
            blk(0), blk(n_pairs), blk(2 * n_pairs),
            pl.BlockSpec((None, s, LANES), lambda i, p: (i, 0, p)),
            pl.BlockSpec((None, s, LANES), lambda i, p: (i, 0, 0)),
            cw(0), cw(n_pairs), cw(2 * n_pairs),
            pl.BlockSpec((None, 2, GATE_ROWS, LANES), lambda i, p: (layer, 0, 0, 0)),
            pl.BlockSpec((None, 1, LANES), lambda i, p: (layer, 0, 0)),
        ],
        out_specs=pl.BlockSpec((None, s, LANES), lambda i, p: (i, 0, p)),
        out_shape=jax.ShapeDtypeStruct((b, s, n_pairs * LANES), BF16),
        scratch_shapes=[
            pltpu.VMEM((s + 16, LANES), F32),
            pltpu.VMEM((s, LANES), BF16),
            pltpu.VMEM((s, LANES), BF16),
            pltpu.VMEM((2, s, LANES), BF16),
            pltpu.VMEM((2, s, LANES), BF16),
            pltpu.VMEM((2, s, 2 * LANES), BF16),
            pltpu.VMEM((2, s, LANES), F32),
            pltpu.VMEM((2, s, LANES), F32),
            pltpu.VMEM((2, s, LANES), F32),
            pltpu.VMEM((4, n_chunks, HEAD_DIM, HEAD_DIM), F32),
            pltpu.VMEM((4, n_chunks, HEAD_DIM, HEAD_DIM), BF16),
            pltpu.VMEM((4, n_chunks, CHUNK, HEAD_DIM), F32),
            pltpu.VMEM((4, n_chunks, CHUNK, HEAD_DIM), BF16),
            pltpu.VMEM((2, s, LANES), F32),
        ],
        compiler_params=pltpu.CompilerParams(
            dimension_semantics=("arbitrary", "arbitrary"), vmem_limit_bytes=VMEM_LIMIT),
        name="deltanet",
    )(gqkv, gqkv, gqkv, gz, gates, conv_w, conv_w, conv_w, gpar, norm_w2)


def _out_proj_kernel(yc_ref, yd_ref, yg_ref, w_ref, x_ref, nw_ref, o_ref):
    c0 = yc_ref.shape[-1]
    c1 = c0 + yd_ref.shape[-1]
    y = (_mm(yc_ref[...], w_ref[0:c0, :]) + _mm(yd_ref[...], w_ref[c0:c1, :])
         + _mm(yg_ref[...], w_ref[c1:, :]))
    o_ref[...] = x_ref[...] + _rms(y, nw_ref[...], NORM_EPS)


def _out_proj(yc, yd, yg, w_all, x2, nw, layer, tm):
    m, d = x2.shape
    row = lambda a: pl.BlockSpec((tm, a.shape[-1]), lambda i: (i, 0))
    return pl.pallas_call(
        _out_proj_kernel,
        grid=(m // tm,),
        in_specs=[
            row(yc), row(yd), row(yg),
            pl.BlockSpec((None, d, d), lambda i: (layer, 0, 0)),
            pl.BlockSpec((tm, d), lambda i: (i, 0)),
            pl.BlockSpec((None, 1, d), lambda i: (layer, 0, 0)),
        ],
        out_specs=pl.BlockSpec((tm, d), lambda i: (i, 0)),
        out_shape=jax.ShapeDtypeStruct((m, d), F32),
        compiler_params=pltpu.CompilerParams(dimension_semantics=("arbitrary",), vmem_limit_bytes=VMEM_LIMIT),
        name="out_proj",
    )(yc, yd, yg, w_all, x2, nw)


def _ffn_kernel(x_ref, nw1_ref, w1_ref, w2_ref, nw2_ref, o_ref, hb_s, acc_s):
    j = pl.program_id(1)

    @pl.when(j == 0)
    def _():
        hb_s[...] = _rms(x_ref[...], nw1_ref[...], NORM_EPS).astype(BF16)

    a = _mm(hb_s[...], w1_ref[...])
    a = jnp.square(jnp.maximum(a, 0.0)).astype(BF16)
    part = _mm(a, w2_ref[...])

    @pl.when(j == 0)
    def _():
        acc_s[...] = part

    @pl.when(j > 0)
    def _():
        acc_s[...] = acc_s[...] + part

    @pl.when(j == pl.num_programs(1) - 1)
    def _():
        o_ref[...] = x_ref[...] + _rms(acc_s[...], nw2_ref[...], NORM_EPS)


def _ffn(x2, nw1, w1_all, w2_all, nw2, layer, tm, tf):
    m, d = x2.shape
    dff = w1_all.shape[-1]
    return pl.pallas_call(
        _ffn_kernel,
        grid=(m // tm, dff // tf),
        in_specs=[
            pl.BlockSpec((tm, d), lambda i, j: (i, 0)),
            pl.BlockSpec((None, 1, d), lambda i, j: (layer, 0, 0)),
            pl.BlockSpec((None, d, tf), lambda i, j: (layer, 0, j)),
            pl.BlockSpec((None, tf, d), lambda i, j: (layer, j, 0)),
            pl.BlockSpec((None, 1, d), lambda i, j: (layer, 0, 0)),
        ],
        out_specs=pl.BlockSpec((tm, d), lambda i, j: (i, 0)),
        out_shape=jax.ShapeDtypeStruct((m, d), F32),
        scratch_shapes=[pltpu.VMEM((tm, d), BF16), pltpu.VMEM((tm, d), F32)],
        compiler_params=pltpu.CompilerParams(
            dimension_semantics=("arbitrary", "arbitrary"), vmem_limit_bytes=VMEM_LIMIT),
        name="ffn",
    )(x2, nw1, w1_all, w2_all, nw2)


def kernel(x, w_in, w_out, pre_mix_w, post_mix_w, pre_mlp_w, post_mlp_w, w_ff1, w_ff2, conv_dw_w, conv_dw_b, conv_ln_w, conv_ln_b, diff_lambda_q1, diff_lambda_k1, diff_lambda_q2, diff_lambda_k2, diff_subln_w, delta_conv_w, delta_A_log, delta_dt_bias, delta_norm_w):
    b, s, d = x.shape
    depth = w_in.shape[0]
    conv_ch = conv_dw_w.shape[-1]
    delta_w = delta_conv_w.shape[-1] // 3
    n_delta_heads = delta_w // HEAD_DIM
    in_w = w_in.shape[-1]
    diff_w = (in_w - 2 * conv_ch - 4 * delta_w - 4 * n_delta_heads) // 3
    n_diff_heads = diff_w // HEAD_DIM
    n_gates = 4 * n_delta_heads
    splits = (2 * conv_ch, 3 * diff_w, 3 * delta_w, delta_w, LANES)
    m = b * s
    tm = min(512, m)

    w_in_b = jnp.pad(w_in, ((0, 0), (0, 0), (0, LANES - n_gates))).astype(BF16)
    w_out_b = w_out.astype(BF16)
    w1_b = w_ff1.astype(BF16)
    w2_b = w_ff2.astype(BF16)
    row3 = lambda a: a.reshape(depth, 1, a.shape[-1])
    lam_all = jnp.stack([diff_lambda_q1, diff_lambda_k1, diff_lambda_q2, diff_lambda_k2], axis=1)
    gate_par = jnp.stack([delta_A_log.reshape(depth, -1), delta_dt_bias.reshape(depth, -1)], axis=1)
    gate_par = jnp.pad(gate_par, ((0, 0), (0, 0), (2 * n_delta_heads, GATE_ROWS - n_gates)))
    gate_par = jnp.broadcast_to(gate_par[..., None], (depth, 2, GATE_ROWS, LANES))
    delta_nw2 = row3(jnp.concatenate([delta_norm_w, delta_norm_w], axis=-1))

    x2 = x.reshape(m, d)
    for l in range(depth):
        uconv, dqkv, gqkv, gz, gates = _in_proj(x2, row3(pre_mix_w), w_in_b, l, splits, tm)
        y_conv = _convmod(uconv.reshape(b, s, -1), conv_dw_w, row3(conv_dw_b), row3(conv_ln_w),
                          row3(conv_ln_b), l)
        y_diff = _diffattn(dqkv.reshape(b, s, -1), lam_all, row3(diff_subln_w), l, n_diff_heads, min(256, s))
        y_delta = _deltanet(gqkv.reshape(b, s, -1), gz.reshape(b, s, -1), gates.reshape(b, s, -1),
                            delta_conv_w, gate_par, delta_nw2, l, n_delta_heads)
        x2 = _out_proj(y_conv.reshape(m, -1), y_diff.reshape(m, -1), y_delta.reshape(m, -1),
                       w_out_b, x2, row3(post_mix_w), l, tm)
        x2 = _ffn(x2, row3(pre_mlp_w), w1_b, w2_b, row3(post_mlp_w), l, min(1024, m), 1024)
    return x2.reshape(b, s, d)
```

```python
import functools
import math

import jax
import jax.numpy as jnp
from jax import lax
from jax.experimental import pallas as pl
from jax.experimental.pallas import tpu as pltpu

F32 = jnp.float32
BF16 = jnp.bfloat16

HEAD_DIM = 64
DIFF_DIM = 32
CONV_WIDTH = 31
SHORT_CONV = 3
CHUNK = 64
NORM_EPS = 1e-6
LANES = 128
GATE_ROWS = 32
VMEM_LIMIT = 56 * 1024 * 1024


def _alibi_slopes(n):
    def pow2(m):
        start = 2.0 ** (-8.0 / m)
        return [start ** (i + 1) for i in range(m)]
    if math.log2(n).is_integer():
        return pow2(n)
    c = 2 ** int(math.floor(math.log2(n)))
    return pow2(c) + pow2(2 * c)[0::2][: n - c]


def _rms(x, w, eps):
    return x * lax.rsqrt(jnp.mean(x * x, axis=-1, keepdims=True) + eps) * w


def _mm(a, b):
    return jnp.dot(a, b, preferred_element_type=F32)


def _mm_nt(a, b):
    return lax.dot_general(a, b, (((1,), (1,)), ((), ())), preferred_element_type=F32)


def _mm_tn(a, b):
    return lax.dot_general(a, b, (((0,), (0,)), ((), ())), preferred_element_type=F32)


def _mm_exact_rhs(x, w):
    hi = x.astype(BF16)
    r1 = x - hi.astype(F32)
    mid = r1.astype(BF16)
    lo = (r1 - mid.astype(F32)).astype(BF16)
    return _mm(hi, w) + _mm(mid, w) + _mm(lo, w)


def _softplus(x):
    return jnp.maximum(x, 0.0) + jnp.log1p(jnp.exp(-jnp.abs(x)))


def _silu(x):
    h = 0.5 * x
    return h + h * jnp.tanh(h)


def _in_proj_kernel(x_ref, nw_ref, w_ref, uconv_ref, dqkv_ref, gqkv_ref, gz_ref, gate_ref, *, splits):
    hb = _rms(x_ref[...], nw_ref[...], NORM_EPS).astype(BF16)
    outs = (uconv_ref, dqkv_ref, gqkv_ref, gz_ref, gate_ref)
    off = 0
    for o_ref, width in zip(outs, splits):
        o_ref[...] = _mm(hb, w_ref[:, off:off + width]).astype(o_ref.dtype)
        off += width


def _in_proj(x2, nw, w_all, layer, splits, tm):
    m, d = x2.shape
    npad = w_all.shape[-1]
    dts = (BF16, BF16, BF16, BF16, F32)
    return pl.pallas_call(
        functools.partial(_in_proj_kernel, splits=splits),
        grid=(m // tm,),
        in_specs=[
            pl.BlockSpec((tm, d), lambda i: (i, 0)),
            pl.BlockSpec((None, 1, d), lambda i: (layer, 0, 0)),
            pl.BlockSpec((None, d, npad), lambda i: (layer, 0, 0)),
        ],
        out_specs=[pl.BlockSpec((tm, w), lambda i: (i, 0)) for w in splits],
        out_shape=[jax.ShapeDtypeStruct((m, w), dt) for w, dt in zip(splits, dts)],
        compiler_params=pltpu.CompilerParams(dimension_semantics=("arbitrary",), vmem_limit_bytes=VMEM_LIMIT),
        name="in_proj",
    )(x2, nw, w_all)


def _convmod_kernel(u_ref, w_ref, b_ref, lnw_ref, lnb_ref, o_ref, pad_ref, *, seq, ch, tile):
    front = 16
    u = u_ref[...].astype(F32)
    hg = 0.5 * u[:, ch:]
    h = u[:, :ch] * (0.5 + 0.5 * jnp.tanh(hg))
    n_cols = ch // LANES
    tail = pad_ref.shape[1] - front - seq
    for c in range(n_cols):
        pad_ref[c, 0:front, :] = jnp.zeros((front, LANES), F32)
        pad_ref[c, front:front + seq, :] = h[:, c * LANES:(c + 1) * LANES]
        pad_ref[c, front + seq:, :] = jnp.zeros((tail, LANES), F32)
    half = (CONV_WIDTH - 1) // 2

    for t in range(seq // tile):
        base = t * tile
        cols = []
        for c in range(n_cols):
            part = jnp.zeros((tile, LANES), F32)
            for k in range(CONV_WIDTH):
                start = base + k + front - half
                part = part + pad_ref[c, start:start + tile, :] * w_ref[k:k + 1, c * LANES:(c + 1) * LANES]
            cols.append(part)
        acc = jnp.concatenate(cols, axis=1) + b_ref[...]
        mu = jnp.mean(acc, axis=-1, keepdims=True)
        xc = acc - mu
        var = jnp.mean(xc * xc, axis=-1, keepdims=True)
        y = xc * lax.rsqrt(var + 1e-5) * lnw_ref[...] + lnb_ref[...]
        o_ref[base:base + tile, :] = _silu(y).astype(o_ref.dtype)


def _convmod(uconv, dw_w, dw_b, ln_w, ln_b, layer):
    b, s, c2 = uconv.shape
    ch = c2 // 2
    tile = 64
    vec = lambda: pl.BlockSpec((None, 1, ch), lambda i: (layer, 0, 0))
    return pl.pallas_call(
        functools.partial(_convmod_kernel, seq=s, ch=ch, tile=tile),
        grid=(b,),
        in_specs=[
            pl.BlockSpec((None, s, c2), lambda i: (i, 0, 0)),
            pl.BlockSpec((None, CONV_WIDTH, ch), lambda i: (layer, 0, 0)),
            vec(), vec(), vec(),
        ],
        out_specs=pl.BlockSpec((None, s, ch), lambda i: (i, 0, 0)),
        out_shape=jax.ShapeDtypeStruct((b, s, ch), BF16),
        scratch_shapes=[pltpu.VMEM((ch // LANES, s + 40, LANES), F32)],
        compiler_params=pltpu.CompilerParams(dimension_semantics=("arbitrary",), vmem_limit_bytes=VMEM_LIMIT),
        name="convmod",
    )(uconv, dw_w, dw_b, ln_w, ln_b)


def _pos_features(pos, slope):
    return slope * (pos & -CHUNK).astype(F32), slope * (pos & (CHUNK - 1)).astype(F32)


def _diffattn_kernel(lam_ref, q_ref, k_ref, v_ref, sw_ref, o_ref, kx_s, vx_s, *, tq, seq, slopes, lambda_init):
    p = pl.program_id(1)
    qi = pl.program_id(2)
    n_kt = seq // tq
    hd, dd = HEAD_DIM, DIFF_DIM
    n_pairs = len(slopes) // 2
    assert all(math.log2(s).is_integer() for s in slopes)

    def head_slope(hh):
        slope = jnp.float32(slopes[2 * (n_pairs - 1) + hh])
        for pp in range(n_pairs - 2, -1, -1):
            slope = jnp.where(p == pp, jnp.float32(slopes[2 * pp + hh]), slope)
        return slope

    def feature_lanes(j, shape):
        lane = lax.broadcasted_iota(jnp.int32, shape, 1)
        data = (lane >= dd * j) & (lane < dd * (j + 1))
        return data, lane - dd * (1 - j)

    @pl.when(qi == 0)
    def _():
        pos = lax.broadcasted_iota(jnp.int32, (seq, hd), 0)
        fl = lax.broadcasted_iota(jnp.int32, (seq, hd), 1) & (dd - 1)
        for hh in range(2):
            f1, f2 = _pos_features(pos, head_slope(hh))
            kh = k_ref[:, hd * hh:hd * (hh + 1)].astype(F32)
            feat = jnp.where(fl < 2, 1.0, jnp.where(fl == 2, f1, jnp.where(fl == 3, f2, 0.0)))
            for j in range(2):
                data, _ = feature_lanes(j, (seq, hd))
                kx_s[0, 2 * hh + j] = jnp.where(data, kh, feat).astype(BF16)
                kx_s[1, 2 * hh + j] = jnp.where(data, kh, -feat).astype(BF16)
            vx_s[hh] = jnp.concatenate([v_ref[:, hd * hh:hd * (hh + 1)], jnp.ones((seq, hd), BF16)], axis=1)

    lp = lam_ref[...]
    lam = (jnp.exp(jnp.sum(lp[0:1] * lp[1:2], axis=-1, keepdims=True))
           - jnp.exp(jnp.sum(lp[2:3] * lp[3:4], axis=-1, keepdims=True)) + lambda_init)
    scale = dd ** -0.5
    qpos = lax.broadcasted_iota(jnp.int32, (tq, hd), 0) + qi * tq
    rr = lax.broadcasted_iota(jnp.int32, (tq, tq), 0)
    cc = lax.broadcasted_iota(jnp.int32, (tq, tq), 1)
    ahead = jnp.maximum(cc - rr, 0).astype(F32)
    tiles = []
    for rel in range(n_kt):
        kt = qi + rel
        kt = jnp.where(kt >= n_kt, kt - n_kt, kt)
        tiles.append((jnp.where(kt > qi, 1, 0), pl.ds(pl.multiple_of(kt * tq, tq), tq)))

    def scores(hh, j):
        slope = head_slope(hh)
        f1, f2 = _pos_features(qpos, slope)
        qh = q_ref[:, hd * hh:hd * (hh + 1)].astype(F32) * scale
        data, fl = feature_lanes(j, (tq, hd))
        feat = jnp.where(fl == 0, -f1, jnp.where(fl == 1, -f2, jnp.where(fl < 4, 1.0, 0.0)))
        qx = jnp.where(data, qh, feat).astype(BF16)
        s_tiles = [_mm_nt(qx, kx_s[side, 2 * hh + j, rows, :]) for side, rows in tiles]
        s_tiles[0] = s_tiles[0] + ahead * (-2.0 * slope)
        return jnp.concatenate(s_tiles, axis=1)

    maps = [(hh, j) for hh in range(2) for j in range(2)]
    s_next = scores(*maps[0])
    outs = []
    for n, (hh, j) in enumerate(maps):
        s = s_next
        if n + 1 < len(maps):
            s_next = scores(*maps[n + 1])
        m = jnp.max(s, axis=-1, keepdims=True)
        e = jnp.exp((s - m).astype(BF16))
        ov = None
        for r, (side, rows) in enumerate(tiles):
            part = _mm(e[:, r * tq:(r + 1) * tq], vx_s[hh, rows, :])
            ov = part if ov is None else ov + part
        outs.append(ov[:, :hd] / ov[:, hd:hd + 1])
        if j == 1:
            o = outs[-2] - lam * outs[-1]
            o = _rms(o, sw_ref[...], 1e-5) * (1.0 - lambda_init)
            o_ref[:, hd * hh:hd * (hh + 1)] = o.astype(o_ref.dtype)


def _diffattn(dqkv, lam_all, subln_w, layer, n_heads, tq):
    b, s, w3 = dqkv.shape
    n_pairs = n_heads // 2
    lambda_init = 0.8 - 0.6 * math.exp(-0.3 * layer)
    slopes = tuple(_alibi_slopes(n_heads))
    return pl.pallas_call(
        functools.partial(_diffattn_kernel, tq=tq, seq=s, slopes=slopes, lambda_init=lambda_init),
        grid=(b, n_pairs, s // tq),
        in_specs=[
            pl.BlockSpec((None, 4, DIFF_DIM), lambda i, p, q: (layer, 0, 0)),
            pl.BlockSpec((None, tq, LANES), lambda i, p, q: (i, q, p)),
            pl.BlockSpec((None, s, LANES), lambda i, p, q: (i, 0, n_pairs + p)),
            pl.BlockSpec((None, s, LANES), lambda i, p, q: (i, 0, 2 * n_pairs + p)),
            pl.BlockSpec((None, 1, HEAD_DIM), lambda i, p, q: (layer, 0, 0)),
        ],
        out_specs=pl.BlockSpec((None, tq, LANES), lambda i, p, q: (i, q, p)),
        out_shape=jax.ShapeDtypeStruct((b, s, n_pairs * LANES), BF16),
        scratch_shapes=[
            pltpu.VMEM((2, 4, s, HEAD_DIM), BF16),
            pltpu.VMEM((2, s, LANES), BF16),
        ],
        compiler_params=pltpu.CompilerParams(
            dimension_semantics=("arbitrary", "arbitrary", "arbitrary"), vmem_limit_bytes=VMEM_LIMIT),
        name="diffattn",
    )(lam_all, dqkv, dqkv, dqkv, subln_w)


def _head_blocks(y, low):
    zero = jnp.zeros_like(y)
    return jnp.concatenate([jnp.where(low, y, zero), jnp.where(low, zero, y)], axis=0)


def _unit_tri_inverses(mats, ii, jj, low):
    eye = (ii == jj).astype(F32)
    diag16 = (ii // 16) == (jj // 16)
    bds = [jnp.where(diag16, -a, 0.0) for a in mats]
    ps = [eye + x for x in bds]
    qs = [_mm(x.astype(BF16), _head_blocks(x.astype(BF16), low)) for x in bds]
    for _ in range(2):
        pqs = [_mm(q.astype(BF16), jnp.concatenate([_head_blocks(p.astype(BF16), low),
                                                    _head_blocks(q.astype(BF16), low)], axis=1))
               for p, q in zip(ps, qs)]
        ps = [p + pq[:, :LANES] for p, pq in zip(ps, pqs)]
        qs = [pq[:, LANES:] for pq in pqs]
    ts = [p + _mm(q.astype(BF16), _head_blocks(p.astype(BF16), low)) for p, q in zip(ps, qs)]
    join16 = ((ii // 32) == (jj // 32)) & ((ii // 16) != (jj // 16))
    tbs = [t.astype(BF16) for t in ts]
    tls = [_mm(tb, _head_blocks(jnp.where(join16, a, 0.0).astype(BF16), low)).astype(BF16) for tb, a in zip(tbs, mats)]
    ts = [t - _mm(tl, _head_blocks(tb, low)) for t, tl, tb in zip(ts, tls, tbs)]
    join32 = (ii // 32) != (jj // 32)
    return ts, [jnp.where(join32, a, 0.0).astype(BF16) for a in mats]


def _deltanet_kernel(gq_ref, gk_ref, gv_ref, z_ref, gate_ref, cwq_ref, cwk_ref, cwv_ref, gpar_ref, nw_ref,
                     o_ref,
                     pad_s, kbf_s, qbf_s, qgx_s, kdec_s, rhs_s, gc_s, gr_s, be_s, gl_s, cc_s, wp_s, op_s, qp_s, osum_s,
                     *, seq, n_gate_heads, n_pairs, group_chunks):
    p = pl.program_id(1)
    n_chunks = seq // CHUNK
    hd = HEAD_DIM

    il = lax.broadcasted_iota(jnp.int32, (LANES, LANES), 0)
    jl = lax.broadcasted_iota(jnp.int32, (LANES, LANES), 1)
    same_head = ((il // hd) == (jl // hd)).astype(BF16)

    edge = jnp.zeros((8, LANES), F32)
    pad_s[0:8, :] = edge
    pad_s[8 + seq:, :] = edge

    def short_conv(x_ref, w_ref):
        pad_s[8:8 + seq, :] = x_ref[...].astype(F32)
        y = (pad_s[7:7 + seq, :] * w_ref[0:1, :] + pad_s[8:8 + seq, :] * w_ref[1:2, :]
             + pad_s[9:9 + seq, :] * w_ref[2:3, :])
        return _silu(y)

    def l2n(x):
        return x * lax.rsqrt(_mm_exact_rhs(x * x, same_head) + 1e-6)

    q = l2n(short_conv(gq_ref, cwq_ref)) * (hd ** -0.5)
    k = l2n(short_conv(gk_ref, cwk_ref))
    v = short_conv(gv_ref, cwv_ref)
    kbf_s[...] = k.astype(BF16)
    qbf_s[...] = q.astype(BF16)

    nh = n_gate_heads
    raw = gate_ref[...].T[0:GATE_ROWS]
    reps = seq // LANES
    a_log = jnp.tile(gpar_ref[0], (1, reps))
    dt_bias = jnp.tile(gpar_ref[1], (1, reps))
    beta = jax.nn.sigmoid(raw)
    g = -jnp.exp(a_log) * _softplus(raw + dt_bias)
    gate_row = lax.broadcasted_iota(jnp.int32, (GATE_ROWS, seq), 0)
    pos_in_chunk = lax.broadcasted_iota(jnp.int32, (GATE_ROWS, seq), 1) & (CHUNK - 1)
    fwd, rev = g, g
    for sh in (1, 2, 4, 8, 16, 32):
        fwd = fwd + jnp.where(pos_in_chunk >= sh, pltpu.roll(fwd, sh, 1), 0.0)
        rev = rev + jnp.where(pos_in_chunk < CHUNK - sh, pltpu.roll(rev, seq - sh, 1), 0.0)
    gc_r = jnp.where(gate_row < 3 * nh, fwd, rev)

    ir = lax.broadcasted_iota(jnp.int32, (GATE_ROWS, 2 * LANES), 0)
    jr = lax.broadcasted_iota(jnp.int32, (GATE_ROWS, 2 * LANES), 1)

    def expand(x, base, pieces):
        sel = (ir == base + nh * (jr // LANES) + 2 * p + (jr % LANES) // hd).astype(BF16)
        out = None
        for _ in range(pieces):
            xb = x.astype(BF16)
            part = _mm_tn(xb, sel)
            out = part if out is None else out + part
            x = x - xb.astype(F32)
        return out

    be_all = expand(beta, 0, 2)
    gc_all = expand(gc_r, 2 * nh, 3)

    for d in range(2):
        dl = slice(LANES * d, LANES * (d + 1))
        be = be_all[:, dl]
        gc = gc_all[:, dl]
        gc3 = gc.reshape(n_chunks, CHUNK, LANES)
        last = CHUNK - 1 if d == 0 else 0
        total3 = jnp.broadcast_to(gc3[:, last:last + 1, :], gc3.shape)
        eg = jnp.exp(gc)
        kb = k * be
        vb = (v * be).astype(BF16)
        kbe = (kb * eg).astype(BF16)
        gc_s[d] = gc
        be_s[d] = be
        gct = gc.T
        row_a = gct[0:8]
        row_b = gct[hd:hd + 8]
        row_a_r = pltpu.roll(row_a, seq - hd, 1)
        row_b_r = pltpu.roll(row_b, hd, 1)
        low8 = lax.broadcasted_iota(jnp.int32, (8, LANES), 1) < hd
        for c in range(n_chunks):
            col = slice(LANES * (c // 2), LANES * (c // 2 + 1))
            if c % 2 == 0:
                gr_s[d, c] = jnp.where(low8, row_a[:, col], row_b_r[:, col])
            else:
                gr_s[d, c] = jnp.where(low8, row_a_r[:, col], row_b[:, col])
        gl_s[2 * p + d] = jnp.exp(gc3[:, last:last + 1, :] + jnp.zeros((n_chunks, 8, LANES), F32))
        qgx_s[d] = pltpu.roll(q * eg, hd, 1).astype(BF16)
        kdec_s[d] = (k * jnp.exp(total3 - gc3).reshape(seq, LANES)).astype(BF16)
        rhs_s[d, :, 0 * hd:1 * hd] = vb[:, :hd]
        rhs_s[d, :, 1 * hd:2 * hd] = kbe[:, :hd]
        rhs_s[d, :, 2 * hd:3 * hd] = kbe[:, hd:]
        rhs_s[d, :, 3 * hd:4 * hd] = vb[:, hd:]

    ii = lax.broadcasted_iota(jnp.int32, (CHUNK, LANES), 0)
    jj = lax.broadcasted_iota(jnp.int32, (CHUNK, LANES), 1) & (CHUNK - 1)

    low = lax.broadcasted_iota(jnp.int32, (CHUNK, LANES), 1) < hd
    zeros_rhs = jnp.zeros((CHUNK, LANES), BF16)

    def two_blocks(x):
        return jnp.concatenate([jnp.concatenate([x[:, :LANES], zeros_rhs], axis=1),
                                jnp.concatenate([zeros_rhs, x[:, LANES:]], axis=1)], axis=0)

    def prep(g, carry):
        keys, a_list, aqk_list = [], [], []
        for ci in range(group_chunks):
            c = g * group_chunks + ci
            rows = pl.ds(pl.multiple_of(c * CHUNK, CHUNK), CHUNK)
            kpk = kbf_s[rows, :]
            kq = _mm_nt(jnp.concatenate([kpk, qbf_s[rows, :]], axis=0), _head_blocks(kpk, low))
            kk, qk = kq[:CHUNK], kq[CHUNK:]
            for d in range(2):
                diff = gc_s[d, rows, :] - jnp.concatenate([gr_s[d, c]] * (CHUNK // 8), axis=0)
                incl = (ii >= jj) if d == 0 else (ii <= jj)
                strict = (ii > jj) if d == 0 else (ii < jj)
                dec = jnp.exp(jnp.where(incl, diff, -1e30))
                a_list.append(jnp.where(strict, kk * dec, 0.0) * be_s[d, rows, :])
                aqk_list.append((qk * dec).astype(BF16))
                keys.append((c, rows, d))
        ts, joins = _unit_tri_inverses(a_list, ii, jj, low)
        tbs = [t.astype(BF16) for t in ts]
        xs = [_mm(tb, two_blocks(rhs_s[d, rows, :])) for tb, (c, rows, d) in zip(tbs, keys)]
        tls = [_mm(tb, _head_blocks(lj, low)).astype(BF16) for tb, lj in zip(tbs, joins)]
        uws = [(x - _mm(tl, two_blocks(x.astype(BF16)))).astype(BF16)
               for x, tl in zip(xs, tls)]
        for uw, aqk, (c, rows, d) in zip(uws, aqk_list, keys):
            kd = _mm_tn(kdec_s[d, rows, :], uw)
            kd_a, kd_b = kd[:hd, :LANES], kd[hd:, LANES:]
            ao = _mm(aqk, two_blocks(uw))
            ao_a, ao_b = ao[:, :LANES], ao[:, LANES:]
            idx = 2 * p + d
            cc_s[idx, c] = jnp.where(low, kd_a, kd_b).astype(BF16)
            wp_s[idx, c] = jnp.where(low, kd_b, kd_a).astype(BF16)
            op_s[idx, c] = jnp.where(low, ao_a, ao_b).astype(BF16)
            qp_s[idx, c] = (qgx_s[d, rows, :].astype(F32) - jnp.where(low, ao_b, ao_a)).astype(BF16)
        return carry

    lax.fori_loop(0, n_chunks // group_chunks, prep, 0)

    @pl.when(p == n_pairs - 1)
    def _():
        def scan(n, states):
            new = []
            for pp in range(n_pairs):
                for d in range(2):
                    idx = 2 * pp + d
                    c = n if d == 0 else n_chunks - 1 - n
                    rows = pl.ds(pl.multiple_of(c * CHUNK, CHUNK), CHUNK)
                    st = states[idx]
                    anti = jnp.concatenate([jnp.where(low, 0.0, st), jnp.where(low, st, 0.0)], axis=0).astype(BF16)
                    osum_s[idx, rows, :] = _mm(qp_s[idx, c], anti) + op_s[idx, c].astype(F32)
                    gl = jnp.concatenate([gl_s[idx, c]] * (hd // 8), axis=0)
                    new.append(st * gl - _mm(wp_s[idx, c], anti) + cc_s[idx, c].astype(F32))
            return tuple(new)

        zero = jnp.zeros((hd, LANES), F32)
        lax.fori_loop(0, n_chunks, scan, (zero,) * (2 * n_pairs))

        for pp in range(n_pairs):
            o = osum_s[2 * pp] + osum_s[2 * pp + 1]
            ms = _mm_exact_rhs(o * o, same_head) * (1.0 / hd)
            zz = z_ref[:, LANES * pp:LANES * (pp + 1)].astype(F32)
            y = o * lax.rsqrt(ms + NORM_EPS) * nw_ref[...] * _silu(zz)
            o_ref[:, LANES * pp:LANES * (pp + 1)] = y.astype(o_ref.dtype)


def _deltanet(gqkv, gz, gates, conv_w, gpar, norm_w2, layer, n_heads):
    b, s, w3 = gqkv.shape
    n_pairs = n_heads // 2
    n_chunks = s // CHUNK
    n_dir = 2 * n_pairs
    blk = lambda off: pl.BlockSpec((None, s, LANES), lambda i, p: (i, 0, off + p))
    cw = lambda off: pl.BlockSpec((None, SHORT_CONV, LANES), lambda i, p: (layer, 0, off + p))
    return pl.pallas_call(
        functools.partial(_deltanet_kernel, seq=s, n_gate_heads=n_heads, n_pairs=n_pairs,
                          group_chunks=min(8, n_chunks)),
        grid=(b, n_pairs),
        in_specs=[
            blk(0), blk(n_pairs), blk(2 * n_pairs),
            pl.BlockSpec((None, s, n_pairs * LANES), lambda i, p: (i, 0, 0)),
            pl.BlockSpec((None, s, LANES), lambda i, p: (i, 0, 0)),
            cw(0), cw(n_pairs), cw(2 * n_pairs),
            pl.BlockSpec((None, 2, GATE_ROWS, LANES), lambda i, p: (layer, 0, 0, 0)),
            pl.BlockSpec((None, 1, LANES), lambda i, p: (layer, 0, 0)),
        ],
        out_specs=pl.BlockSpec((None, s, n_pairs * LANES), lambda i, p: (i, 0, 0)),
        out_shape=jax.ShapeDtypeStruct((b, s, n_pairs * LANES), BF16),
        scratch_shapes=[
            pltpu.VMEM((s + 16, LANES), F32),
            pltpu.VMEM((s, LANES), BF16),
            pltpu.VMEM((s, LANES), BF16),
            pltpu.VMEM((2, s, LANES), BF16),
            pltpu.VMEM((2, s, LANES), BF16),
            pltpu.VMEM((2, s, 2 * LANES), BF16),
            pltpu.VMEM((2, s, LANES), F32),
            pltpu.VMEM((2, n_chunks, 8, LANES), F32),
            pltpu.VMEM((2, s, LANES), F32),
            pltpu.VMEM((n_dir, n_chunks, 8, LANES), F32),
            pltpu.VMEM((n_dir, n_chunks, HEAD_DIM, LANES), BF16),
            pltpu.VMEM((n_dir, n_chunks, HEAD_DIM, LANES), BF16),
            pltpu.VMEM((n_dir, n_chunks, CHUNK, LANES), BF16),
            pltpu.VMEM((n_dir, n_chunks, CHUNK, LANES), BF16),
            pltpu.VMEM((n_dir, s, LANES), F32),
        ],
        compiler_params=pltpu.CompilerParams(
            dimension_semantics=("arbitrary", "arbitrary"), vmem_limit_bytes=VMEM_LIMIT),
        name="deltanet",
    )(gqkv, gqkv, gqkv, gz, gates, conv_w, conv_w, conv_w, gpar, norm_w2)


def _out_proj_kernel(yc_ref, yd_ref, yg_ref, w_ref, x_ref, nw_ref, o_ref):
    c0 = yc_ref.shape[-1]
    c1 = c0 + yd_ref.shape[-1]
    y = (_mm(yc_ref[...], w_ref[0:c0, :]) + _mm(yd_ref[...], w_ref[c0:c1, :])
         + _mm(yg_ref[...], w_ref[c1:, :]))
    o_ref[...] = x_ref[...] + _rms(y, nw_ref[...], NORM_EPS)


def _out_proj(yc, yd, yg, w_all, x2, nw, layer, tm):
    m, d = x2.shape
    row = lambda a: pl.BlockSpec((tm, a.shape[-1]), lambda i: (i, 0))
    return pl.pallas_call(
        _out_proj_kernel,
        grid=(m // tm,),
        in_specs=[
            row(yc), row(yd), row(yg),
            pl.BlockSpec((None, d, d), lambda i: (layer, 0, 0)),
            pl.BlockSpec((tm, d), lambda i: (i, 0)),
            pl.BlockSpec((None, 1, d), lambda i: (layer, 0, 0)),
        ],
        out_specs=pl.BlockSpec((tm, d), lambda i: (i, 0)),
        out_shape=jax.ShapeDtypeStruct((m, d), F32),
        compiler_params=pltpu.CompilerParams(dimension_semantics=("arbitrary",), vmem_limit_bytes=VMEM_LIMIT),
        name="out_proj",
    )(yc, yd, yg, w_all, x2, nw)


def _ffn_kernel(x_ref, nw1_ref, w1_ref, w2_ref, nw2_ref, o_ref, hb_s, acc_s):
    j = pl.program_id(1)

    @pl.when(j == 0)
    def _():
        hb_s[...] = _rms(x_ref[...], nw1_ref[...], NORM_EPS).astype(BF16)

    a = _mm(hb_s[...], w1_ref[...])
    a = jnp.square(jnp.maximum(a, 0.0)).astype(BF16)
    part = _mm(a, w2_ref[...])

    @pl.when(j == 0)
    def _():
        acc_s[...] = part

    @pl.when(j > 0)
    def _():
        acc_s[...] = acc_s[...] + part

    @pl.when(j == pl.num_programs(1) - 1)
    def _():
        o_ref[...] = x_ref[...] + _rms(acc_s[...], nw2_ref[...], NORM_EPS)


def _ffn(x2, nw1, w1_all, w2_all, nw2, layer, tm, tf):
    m, d = x2.shape
    dff = w1_all.shape[-1]
    return pl.pallas_call(
        _ffn_kernel,
        grid=(m // tm, dff // tf),
        in_specs=[
            pl.BlockSpec((tm, d), lambda i, j: (i, 0)),
            pl.BlockSpec((None, 1, d), lambda i, j: (layer, 0, 0)),
            pl.BlockSpec((None, d, tf), lambda i, j: (layer, 0, j)),
            pl.BlockSpec((None, tf, d), lambda i, j: (layer, j, 0)),
            pl.BlockSpec((None, 1, d), lambda i, j: (layer, 0, 0)),
        ],
        out_specs=pl.BlockSpec((tm, d), lambda i, j: (i, 0)),
        out_shape=jax.ShapeDtypeStruct((m, d), F32),
        scratch_shapes=[pltpu.VMEM((tm, d), BF16), pltpu.VMEM((tm, d), F32)],
        compiler_params=pltpu.CompilerParams(
            dimension_semantics=("arbitrary", "arbitrary"), vmem_limit_bytes=VMEM_LIMIT),
        name="ffn",
    )(x2, nw1, w1_all, w2_all, nw2)


def kernel(x, w_in, w_out, pre_mix_w, post_mix_w, pre_mlp_w, post_mlp_w, w_ff1, w_ff2, conv_dw_w, conv_dw_b, conv_ln_w, conv_ln_b, diff_lambda_q1, diff_lambda_k1, diff_lambda_q2, diff_lambda_k2, diff_subln_w, delta_conv_w, delta_A_log, delta_dt_bias, delta_norm_w):
    b, s, d = x.shape
    depth = w_in.shape[0]
    conv_ch = conv_dw_w.shape[-1]
    delta_w = delta_conv_w.shape[-1] // 3
    n_delta_heads = delta_w // HEAD_DIM
    in_w = w_in.shape[-1]
    diff_w = (in_w - 2 * conv_ch - 4 * delta_w - 4 * n_delta_heads) // 3
    n_diff_heads = diff_w // HEAD_DIM
    n_gates = 4 * n_delta_heads
    splits = (2 * conv_ch, 3 * diff_w, 3 * delta_w, delta_w, LANES)
    m = b * s
    tm = min(512, m)

    w_in_b = jnp.pad(w_in, ((0, 0), (0, 0), (0, LANES - n_gates))).astype(BF16)
    w_out_b = w_out.astype(BF16)
    w1_b = w_ff1.astype(BF16)
    w2_b = w_ff2.astype(BF16)
    row3 = lambda a: a.reshape(depth, 1, a.shape[-1])
    lam_all = jnp.stack([diff_lambda_q1, diff_lambda_k1, diff_lambda_q2, diff_lambda_k2], axis=1)
    gate_par = jnp.stack([delta_A_log.reshape(depth, -1), delta_dt_bias.reshape(depth, -1)], axis=1)
    gate_par = jnp.pad(gate_par, ((0, 0), (0, 0), (2 * n_delta_heads, GATE_ROWS - n_gates)))
    gate_par = jnp.broadcast_to(gate_par[..., None], (depth, 2, GATE_ROWS, LANES))
    delta_nw2 = row3(jnp.concatenate([delta_norm_w, delta_norm_w], axis=-1))

    x2 = x.reshape(m, d)
    for l in range(depth):
        uconv, dqkv, gqkv, gz, gates = _in_proj(x2, row3(pre_mix_w), w_in_b, l, splits, tm)
        y_conv = _convmod(uconv.reshape(b, s, -1), conv_dw_w, row3(conv_dw_b), row3(conv_ln_w),
                          row3(conv_ln_b), l)
        y_diff = _diffattn(dqkv.reshape(b, s, -1), lam_all, row3(diff_subln_w), l, n_diff_heads, min(256, s))
        y_delta = _deltanet(gqkv.reshape(b, s, -1), gz.reshape(b, s, -1), gates.reshape(b, s, -1),
                            delta_conv_w, gate_par, delta_nw2, l, n_delta_heads)
        x2 = _out_proj(y_conv.reshape(m, -1), y_diff.reshape(m, -1), y_delta.reshape(m, -1),
                       w_out_b, x2, row3(post_mix_w), l, tm)
        x2 = _ffn(x2, row3(pre_mlp_w), w1_b, w2_b, row3(post_mlp_w), l, min(1024, m), 1024)
    return x2.reshape(b, s, d)
```

```python
import functools
import math

import jax
import jax.numpy as jnp
from jax import lax
from jax.experimental import pallas as pl
from jax.experimental.pallas import tpu as pltpu

F32 = jnp.float32
BF16 = jnp.bfloat16

HEAD_DIM = 64
DIFF_DIM = 32
CONV_WIDTH = 31
SHORT_CONV = 3
CHUNK = 64
NORM_EPS = 1e-6
LANES = 128
GATE_ROWS = 32
VMEM_LIMIT = 56 * 1024 * 1024


def _alibi_slopes(n):
    def pow2(m):
        start = 2.0 ** (-8.0 / m)
        return [start ** (i + 1) for i in range(m)]
    if math.log2(n).is_integer():
        return pow2(n)
    c = 2 ** int(math.floor(math.log2(n)))
    return pow2(c) + pow2(2 * c)[0::2][: n - c]


def _rms(x, w, eps):
    return x * lax.rsqrt(jnp.mean(x * x, axis=-1, keepdims=True) + eps) * w


def _mm(a, b):
    return jnp.dot(a, b, preferred_element_type=F32)


def _mm_nt(a, b):
    return lax.dot_general(a, b, (((1,), (1,)), ((), ())), preferred_element_type=F32)


def _mm_tn(a, b):
    return lax.dot_general(a, b, (((0,), (0,)), ((), ())), preferred_element_type=F32)


def _mm_exact_rhs(x, w):
    hi = x.astype(BF16)
    r1 = x - hi.astype(F32)
    mid = r1.astype(BF16)
    lo = (r1 - mid.astype(F32)).astype(BF16)
    return _mm(hi, w) + _mm(mid, w) + _mm(lo, w)


def _softplus(x):
    return jnp.maximum(x, 0.0) + jnp.log1p(jnp.exp(-jnp.abs(x)))


def _silu(x):
    h = 0.5 * x
    return h + h * jnp.tanh(h)


def _in_proj_kernel(x_ref, nw_ref, w_ref, uconv_ref, dqkv_ref, gqkv_ref, gz_ref, gate_ref, *, splits):
    hb = _rms(x_ref[...], nw_ref[...], NORM_EPS).astype(BF16)
    outs = (uconv_ref, dqkv_ref, gqkv_ref, gz_ref, gate_ref)
    off = 0
    for o_ref, width in zip(outs, splits):
        o_ref[...] = _mm(hb, w_ref[:, off:off + width]).astype(o_ref.dtype)
        off += width


def _in_proj(x2, nw, w_all, layer, splits, tm):
    m, d = x2.shape
    npad = w_all.shape[-1]
    dts = (BF16, BF16, BF16, BF16, F32)
    return pl.pallas_call(
        functools.partial(_in_proj_kernel, splits=splits),
        grid=(m // tm,),
        in_specs=[
            pl.BlockSpec((tm, d), lambda i: (i, 0)),
            pl.BlockSpec((None, 1, d), lambda i: (layer, 0, 0)),
            pl.BlockSpec((None, d, npad), lambda i: (layer, 0, 0)),
        ],
        out_specs=[pl.BlockSpec((tm, w), lambda i: (i, 0)) for w in splits],
        out_shape=[jax.ShapeDtypeStruct((m, w), dt) for w, dt in zip(splits, dts)],
        compiler_params=pltpu.CompilerParams(dimension_semantics=("arbitrary",), vmem_limit_bytes=VMEM_LIMIT),
        name="in_proj",
    )(x2, nw, w_all)


def _convmod_kernel(u_ref, w_ref, b_ref, lnw_ref, lnb_ref, o_ref, pad_ref, *, seq, ch, tile):
    front = 16
    u = u_ref[...].astype(F32)
    hg = 0.5 * u[:, ch:]
    h = u[:, :ch] * (0.5 + 0.5 * jnp.tanh(hg))
    n_cols = ch // LANES
    tail = pad_ref.shape[1] - front - seq
    for c in range(n_cols):
        pad_ref[c, 0:front, :] = jnp.zeros((front, LANES), F32)
        pad_ref[c, front:front + seq, :] = h[:, c * LANES:(c + 1) * LANES]
        pad_ref[c, front + seq:, :] = jnp.zeros((tail, LANES), F32)
    half = (CONV_WIDTH - 1) // 2

    for t in range(seq // tile):
        base = t * tile
        cols = []
        for c in range(n_cols):
            part = jnp.zeros((tile, LANES), F32)
            for k in range(CONV_WIDTH):
                start = base + k + front - half
                part = part + pad_ref[c, start:start + tile, :] * w_ref[k:k + 1, c * LANES:(c + 1) * LANES]
            cols.append(part)
        acc = jnp.concatenate(cols, axis=1) + b_ref[...]
        mu = jnp.mean(acc, axis=-1, keepdims=True)
        xc = acc - mu
        var = jnp.mean(xc * xc, axis=-1, keepdims=True)
        y = xc * lax.rsqrt(var + 1e-5) * lnw_ref[...] + lnb_ref[...]
        o_ref[base:base + tile, :] = _silu(y).astype(o_ref.dtype)


def _convmod(uconv, dw_w, dw_b, ln_w, ln_b, layer):
    b, s, c2 = uconv.shape
    ch = c2 // 2
    tile = 64
    vec = lambda: pl.BlockSpec((None, 1, ch), lambda i: (layer, 0, 0))
    return pl.pallas_call(
        functools.partial(_convmod_kernel, seq=s, ch=ch, tile=tile),
        grid=(b,),
        in_specs=[
            pl.BlockSpec((None, s, c2), lambda i: (i, 0, 0)),
            pl.BlockSpec((None, CONV_WIDTH, ch), lambda i: (layer, 0, 0)),
            vec(), vec(), vec(),
        ],
        out_specs=pl.BlockSpec((None, s, ch), lambda i: (i, 0, 0)),
        out_shape=jax.ShapeDtypeStruct((b, s, ch), BF16),
        scratch_shapes=[pltpu.VMEM((ch // LANES, s + 40, LANES), F32)],
        compiler_params=pltpu.CompilerParams(dimension_semantics=("arbitrary",), vmem_limit_bytes=VMEM_LIMIT),
        name="convmod",
    )(uconv, dw_w, dw_b, ln_w, ln_b)


def _pos_features(pos, slope):
    return slope * (pos & -CHUNK).astype(F32), slope * (pos & (CHUNK - 1)).astype(F32)


def _diffattn_kernel(lam_ref, q_ref, k_ref, v_ref, sw_ref, o_ref, kx_s, vx_s, *, tq, seq, slopes, lambda_init):
    p = pl.program_id(1)
    qi = pl.program_id(2)
    n_kt = seq // tq
    hd, dd = HEAD_DIM, DIFF_DIM
    n_pairs = len(slopes) // 2
    assert all(math.log2(s).is_integer() for s in slopes)

    def head_slope(hh):
        slope = jnp.float32(slopes[2 * (n_pairs - 1) + hh])
        for pp in range(n_pairs - 2, -1, -1):
            slope = jnp.where(p == pp, jnp.float32(slopes[2 * pp + hh]), slope)
        return slope

    def feature_lanes(j, shape):
        lane = lax.broadcasted_iota(jnp.int32, shape, 1)
        data = (lane >= dd * j) & (lane < dd * (j + 1))
        return data, lane - dd * (1 - j)

    @pl.when(qi == 0)
    def _():
        pos = lax.broadcasted_iota(jnp.int32, (seq, hd), 0)
        fl = lax.broadcasted_iota(jnp.int32, (seq, hd), 1) & (dd - 1)
        for hh in range(2):
            f1, f2 = _pos_features(pos, head_slope(hh))
            kh = k_ref[:, hd * hh:hd * (hh + 1)].astype(F32)
            feat = jnp.where(fl < 2, 1.0, jnp.where(fl == 2, f1, jnp.where(fl == 3, f2, 0.0)))
            for j in range(2):
                data, _ = feature_lanes(j, (seq, hd))
                kx_s[0, 2 * hh + j] = jnp.where(data, kh, feat).astype(BF16)
                kx_s[1, 2 * hh + j] = jnp.where(data, kh, -feat).astype(BF16)
            vx_s[hh] = jnp.concatenate([v_ref[:, hd * hh:hd * (hh + 1)], jnp.ones((seq, hd), BF16)], axis=1)

    lp = lam_ref[...]
    lam = (jnp.exp(jnp.sum(lp[0:1] * lp[1:2], axis=-1, keepdims=True))
           - jnp.exp(jnp.sum(lp[2:3] * lp[3:4], axis=-1, keepdims=True)) + lambda_init)
    scale = dd ** -0.5
    qpos = lax.broadcasted_iota(jnp.int32, (tq, hd), 0) + qi * tq
    rr = lax.broadcasted_iota(jnp.int32, (tq, tq), 0)
    cc = lax.broadcasted_iota(jnp.int32, (tq, tq), 1)
    ahead = jnp.maximum(cc - rr, 0).astype(F32)
    tiles = []
    for rel in range(n_kt):
        kt = qi + rel
        kt = jnp.where(kt >= n_kt, kt - n_kt, kt)
        tiles.append((jnp.where(kt > qi, 1, 0), pl.ds(pl.multiple_of(kt * tq, tq), tq)))

    def scores(hh, j):
        slope = head_slope(hh)
        f1, f2 = _pos_features(qpos, slope)
        qh = q_ref[:, hd * hh:hd * (hh + 1)].astype(F32) * scale
        data, fl = feature_lanes(j, (tq, hd))
        feat = jnp.where(fl == 0, -f1, jnp.where(fl == 1, -f2, jnp.where(fl < 4, 1.0, 0.0)))
        qx = jnp.where(data, qh, feat).astype(BF16)
        s_tiles = [_mm_nt(qx, kx_s[side, 2 * hh + j, rows, :]) for side, rows in tiles]
        s_tiles[0] = s_tiles[0] + ahead * (-2.0 * slope)
        return jnp.concatenate(s_tiles, axis=1)

    maps = [(hh, j) for hh in range(2) for j in range(2)]
    s_next = scores(*maps[0])
    outs = []
    for n, (hh, j) in enumerate(maps):
        s = s_next
        if n + 1 < len(maps):
            s_next = scores(*maps[n + 1])
        m = jnp.max(s, axis=-1, keepdims=True)
        e = jnp.exp((s - m).astype(BF16))
        ov = None
        for r, (side, rows) in enumerate(tiles):
            part = _mm(e[:, r * tq:(r + 1) * tq], vx_s[hh, rows, :])
            ov = part if ov is None else ov + part
        outs.append(ov[:, :hd] / ov[:, hd:hd + 1])
        if j == 1:
            o = outs[-2] - lam * outs[-1]
            o = _rms(o, sw_ref[...], 1e-5) * (1.0 - lambda_init)
            o_ref[:, hd * hh:hd * (hh + 1)] = o.astype(o_ref.dtype)


def _diffattn(dqkv, lam_all, subln_w, layer, n_heads, tq):
    b, s, w3 = dqkv.shape
    n_pairs = n_heads // 2
    lambda_init = 0.8 - 0.6 * math.exp(-0.3 * layer)
    slopes = tuple(_alibi_slopes(n_heads))
    return pl.pallas_call(
        functools.partial(_diffattn_kernel, tq=tq, seq=s, slopes=slopes, lambda_init=lambda_init),
        grid=(b, n_pairs, s // tq),
        in_specs=[
            pl.BlockSpec((None, 4, DIFF_DIM), lambda i, p, q: (layer, 0, 0)),
            pl.BlockSpec((None, tq, LANES), lambda i, p, q: (i, q, p)),
            pl.BlockSpec((None, s, LANES), lambda i, p, q: (i, 0, n_pairs + p)),
            pl.BlockSpec((None, s, LANES), lambda i, p, q: (i, 0, 2 * n_pairs + p)),
            pl.BlockSpec((None, 1, HEAD_DIM), lambda i, p, q: (layer, 0, 0)),
        ],
        out_specs=pl.BlockSpec((None, tq, LANES), lambda i, p, q: (i, q, p)),
        out_shape=jax.ShapeDtypeStruct((b, s, n_pairs * LANES), BF16),
        scratch_shapes=[
            pltpu.VMEM((2, 4, s, HEAD_DIM), BF16),
            pltpu.VMEM((2, s, LANES), BF16),
        ],
        compiler_params=pltpu.CompilerParams(
            dimension_semantics=("arbitrary", "arbitrary", "arbitrary"), vmem_limit_bytes=VMEM_LIMIT),
        name="diffattn",
    )(lam_all, dqkv, dqkv, dqkv, subln_w)


def _head_blocks(y, low):
    zero = jnp.zeros_like(y)
    return jnp.concatenate([jnp.where(low, y, zero), jnp.where(low, zero, y)], axis=0)


def _unit_tri_inverses(mats, ii, jj, low):
    eye = (ii == jj).astype(F32)
    diag16 = (ii // 16) == (jj // 16)
    bds = [jnp.where(diag16, -a, 0.0) for a in mats]
    ps = [eye + x for x in bds]
    qs = [_mm(x.astype(BF16), _head_blocks(x.astype(BF16), low)) for x in bds]
    for _ in range(2):
        pqs = [_mm(q.astype(BF16), jnp.concatenate([_head_blocks(p.astype(BF16), low),
                                                    _head_blocks(q.astype(BF16), low)], axis=1))
               for p, q in zip(ps, qs)]
        ps = [p + pq[:, :LANES] for p, pq in zip(ps, pqs)]
        qs = [pq[:, LANES:] for pq in pqs]
    ts = [p + _mm(q.astype(BF16), _head_blocks(p.astype(BF16), low)) for p, q in zip(ps, qs)]
    join16 = ((ii // 32) == (jj // 32)) & ((ii // 16) != (jj // 16))
    tbs = [t.astype(BF16) for t in ts]
    tls = [_mm(tb, _head_blocks(jnp.where(join16, a, 0.0).astype(BF16), low)).astype(BF16) for tb, a in zip(tbs, mats)]
    ts = [t - _mm(tl, _head_blocks(tb, low)) for t, tl, tb in zip(ts, tls, tbs)]
    join32 = (ii // 32) != (jj // 32)
    return ts, [jnp.where(join32, a, 0.0).astype(BF16) for a in mats]


def _deltanet_kernel(gq_ref, gk_ref, gv_ref, z_ref, gate_ref, cwq_ref, cwk_ref, cwv_ref, gpar_ref, nw_ref,
                     o_ref,
                     pad_s, gate_s, kbf_s, qbf_s, qgx_s, kdec_s, rhs_s, gc_s, gr_s, be_s, gl_s, cc_s, wp_s, op_s, qp_s,
                     osum_s,
                     *, seq, n_gate_heads, n_pairs, group_chunks):
    p = pl.program_id(1)
    n_chunks = seq // CHUNK
    hd = HEAD_DIM

    il = lax.broadcasted_iota(jnp.int32, (LANES, LANES), 0)
    jl = lax.broadcasted_iota(jnp.int32, (LANES, LANES), 1)
    same_head = ((il // hd) == (jl // hd)).astype(BF16)

    edge = jnp.zeros((8, LANES), F32)
    pad_s[0:8, :] = edge
    pad_s[8 + seq:, :] = edge

    def short_conv(x_ref, w_ref):
        pad_s[8:8 + seq, :] = x_ref[...].astype(F32)
        y = (pad_s[7:7 + seq, :] * w_ref[0:1, :] + pad_s[8:8 + seq, :] * w_ref[1:2, :]
             + pad_s[9:9 + seq, :] * w_ref[2:3, :])
        return _silu(y)

    def l2n(x):
        return x * lax.rsqrt(_mm_exact_rhs(x * x, same_head) + 1e-6)

    q = l2n(short_conv(gq_ref, cwq_ref)) * (hd ** -0.5)
    k = l2n(short_conv(gk_ref, cwk_ref))
    v = short_conv(gv_ref, cwv_ref)
    kbf_s[...] = k.astype(BF16)
    qbf_s[...] = q.astype(BF16)

    nh = n_gate_heads

    @pl.when(p == 0)
    def _():
        raw = gate_ref[...].T[0:GATE_ROWS]
        reps = seq // LANES
        a_log = jnp.tile(gpar_ref[0], (1, reps))
        dt_bias = jnp.tile(gpar_ref[1], (1, reps))
        g = -jnp.exp(a_log) * _softplus(raw + dt_bias)
        gate_row = lax.broadcasted_iota(jnp.int32, (GATE_ROWS, seq), 0)
        pos_in_chunk = lax.broadcasted_iota(jnp.int32, (GATE_ROWS, seq), 1) & (CHUNK - 1)
        fwd, rev = g, g
        for sh in (1, 2, 4, 8, 16, 32):
            fwd = fwd + jnp.where(pos_in_chunk >= sh, pltpu.roll(fwd, sh, 1), 0.0)
            rev = rev + jnp.where(pos_in_chunk < CHUNK - sh, pltpu.roll(rev, seq - sh, 1), 0.0)
        gate_s[0] = jax.nn.sigmoid(raw)
        gate_s[1] = jnp.where(gate_row < 3 * nh, fwd, rev)

    beta = gate_s[0]
    gc_r = gate_s[1]

    ir = lax.broadcasted_iota(jnp.int32, (GATE_ROWS, 2 * LANES), 0)
    jr = lax.broadcasted_iota(jnp.int32, (GATE_ROWS, 2 * LANES), 1)

    def expand(x, base, pieces):
        sel = (ir == base + nh * (jr // LANES) + 2 * p + (jr % LANES) // hd).astype(BF16)
        out = None
        for _ in range(pieces):
            xb = x.astype(BF16)
            part = _mm_tn(xb, sel)
            out = part if out is None else out + part
            x = x - xb.astype(F32)
        return out

    be_all = expand(beta, 0, 2)
    gc_all = expand(gc_r, 2 * nh, 3)

    for d in range(2):
        dl = slice(LANES * d, LANES * (d + 1))
        be = be_all[:, dl]
        gc = gc_all[:, dl]
        gc3 = gc.reshape(n_chunks, CHUNK, LANES)
        last = CHUNK - 1 if d == 0 else 0
        total3 = jnp.broadcast_to(gc3[:, last:last + 1, :], gc3.shape)
        eg = jnp.exp(gc)
        kb = k * be
        vb = (v * be).astype(BF16)
        kbe = (kb * eg).astype(BF16)
        gc_s[d] = gc
        be_s[d] = be
        gct = gc.T
        row_a = gct[0:8]
        row_b = gct[hd:hd + 8]
        row_a_r = pltpu.roll(row_a, seq - hd, 1)
        row_b_r = pltpu.roll(row_b, hd, 1)
        low8 = lax.broadcasted_iota(jnp.int32, (8, LANES), 1) < hd
        for c in range(n_chunks):
            col = slice(LANES * (c // 2), LANES * (c // 2 + 1))
            if c % 2 == 0:
                gr_s[d, c] = jnp.where(low8, row_a[:, col], row_b_r[:, col])
            else:
                gr_s[d, c] = jnp.where(low8, row_a_r[:, col], row_b[:, col])
        gl_s[2 * p + d] = jnp.exp(gc3[:, last:last + 1, :] + jnp.zeros((n_chunks, 8, LANES), F32))
        qgx_s[d] = pltpu.roll(q * eg, hd, 1).astype(BF16)
        kdec_s[d] = (k * jnp.exp(total3 - gc3).reshape(seq, LANES)).astype(BF16)
        rhs_s[d, :, 0 * hd:1 * hd] = vb[:, :hd]
        rhs_s[d, :, 1 * hd:2 * hd] = kbe[:, :hd]
        rhs_s[d, :, 2 * hd:3 * hd] = kbe[:, hd:]
        rhs_s[d, :, 3 * hd:4 * hd] = vb[:, hd:]

    ii = lax.broadcasted_iota(jnp.int32, (CHUNK, LANES), 0)
    jj = lax.broadcasted_iota(jnp.int32, (CHUNK, LANES), 1) & (CHUNK - 1)

    low = lax.broadcasted_iota(jnp.int32, (CHUNK, LANES), 1) < hd
    zeros_rhs = jnp.zeros((CHUNK, LANES), BF16)

    def two_blocks(x):
        return jnp.concatenate([jnp.concatenate([x[:, :LANES], zeros_rhs], axis=1),
                                jnp.concatenate([zeros_rhs, x[:, LANES:]], axis=1)], axis=0)

    def prep(g, carry):
        keys, a_list, aqk_list = [], [], []
        for ci in range(group_chunks):
            c = g * group_chunks + ci
            rows = pl.ds(pl.multiple_of(c * CHUNK, CHUNK), CHUNK)
            kpk = kbf_s[rows, :]
            kq = _mm_nt(jnp.concatenate([kpk, qbf_s[rows, :]], axis=0), _head_blocks(kpk, low))
            kk, qk = kq[:CHUNK], kq[CHUNK:]
            for d in range(2):
                diff = gc_s[d, rows, :] - jnp.concatenate([gr_s[d, c]] * (CHUNK // 8), axis=0)
                incl = (ii >= jj) if d == 0 else (ii <= jj)
                strict = (ii > jj) if d == 0 else (ii < jj)
                dec = jnp.exp(jnp.where(incl, diff, -1e30))
                a_list.append(jnp.where(strict, kk * dec, 0.0) * be_s[d, rows, :])
                aqk_list.append((qk * dec).astype(BF16))
                keys.append((c, rows, d))
        ts, joins = _unit_tri_inverses(a_list, ii, jj, low)
        tbs = [t.astype(BF16) for t in ts]
        xs = [_mm(tb, two_blocks(rhs_s[d, rows, :])) for tb, (c, rows, d) in zip(tbs, keys)]
        tls = [_mm(tb, _head_blocks(lj, low)).astype(BF16) for tb, lj in zip(tbs, joins)]
        uws = [(x - _mm(tl, two_blocks(x.astype(BF16)))).astype(BF16)
               for x, tl in zip(xs, tls)]
        for uw, aqk, (c, rows, d) in zip(uws, aqk_list, keys):
            kd = _mm_tn(kdec_s[d, rows, :], uw)
            kd_a, kd_b = kd[:hd, :LANES], kd[hd:, LANES:]
            ao = _mm(aqk, two_blocks(uw))
            ao_a, ao_b = ao[:, :LANES], ao[:, LANES:]
            idx = 2 * p + d
            cc_s[idx, c] = jnp.where(low, kd_a, kd_b).astype(BF16)
            wp_s[idx, c] = jnp.where(low, kd_b, kd_a).astype(BF16)
            op_s[idx, c] = jnp.where(low, ao_a, ao_b).astype(BF16)
            qp_s[idx, c] = (qgx_s[d, rows, :].astype(F32) - jnp.where(low, ao_b, ao_a)).astype(BF16)
        return carry

    lax.fori_loop(0, n_chunks // group_chunks, prep, 0)

    @pl.when(p == n_pairs - 1)
    def _():
        def scan(n, states):
            new = []
            for pp in range(n_pairs):
                for d in range(2):
                    idx = 2 * pp + d
                    c = n if d == 0 else n_chunks - 1 - n
                    rows = pl.ds(pl.multiple_of(c * CHUNK, CHUNK), CHUNK)
                    st = states[idx]
                    anti = jnp.concatenate([jnp.where(low, 0.0, st), jnp.where(low, st, 0.0)], axis=0).astype(BF16)
                    osum_s[idx, rows, :] = _mm(qp_s[idx, c], anti) + op_s[idx, c].astype(F32)
                    gl = jnp.concatenate([gl_s[idx, c]] * (hd // 8), axis=0)
                    new.append(st * gl - _mm(wp_s[idx, c], anti) + cc_s[idx, c].astype(F32))
            return tuple(new)

        zero = jnp.zeros((hd, LANES), F32)
        lax.fori_loop(0, n_chunks, scan, (zero,) * (2 * n_pairs))

        for pp in range(n_pairs):
            o = osum_s[2 * pp] + osum_s[2 * pp + 1]
            ms = _mm_exact_rhs(o * o, same_head) * (1.0 / hd)
            zz = z_ref[:, LANES * pp:LANES * (pp + 1)].astype(F32)
            y = o * lax.rsqrt(ms + NORM_EPS) * nw_ref[...] * _silu(zz)
            o_ref[:, LANES * pp:LANES * (pp + 1)] = y.astype(o_ref.dtype)


def _deltanet(gqkv, gz, gates, conv_w, gpar, norm_w2, layer, n_heads):
    b, s, w3 = gqkv.shape
    n_pairs = n_heads // 2
    n_chunks = s // CHUNK
    n_dir = 2 * n_pairs
    blk = lambda off: pl.BlockSpec((None, s, LANES), lambda i, p: (i, 0, off + p))
    cw = lambda off: pl.BlockSpec((None, SHORT_CONV, LANES), lambda i, p: (layer, 0, off + p))
    return pl.pallas_call(
        functools.partial(_deltanet_kernel, seq=s, n_gate_heads=n_heads, n_pairs=n_pairs,
                          group_chunks=min(8, n_chunks)),
        grid=(b, n_pairs),
        in_specs=[
            blk(0), blk(n_pairs), blk(2 * n_pairs),
            pl.BlockSpec((None, s, n_pairs * LANES), lambda i, p: (i, 0, 0)),
            pl.BlockSpec((None, s, LANES), lambda i, p: (i, 0, 0)),
            cw(0), cw(n_pairs), cw(2 * n_pairs),
            pl.BlockSpec((None, 2, GATE_ROWS, LANES), lambda i, p: (layer, 0, 0, 0)),
            pl.BlockSpec((None, 1, LANES), lambda i, p: (layer, 0, 0)),
        ],
        out_specs=pl.BlockSpec((None, s, n_pairs * LANES), lambda i, p: (i, 0, 0)),
        out_shape=jax.ShapeDtypeStruct((b, s, n_pairs * LANES), BF16),
        scratch_shapes=[
            pltpu.VMEM((s + 16, LANES), F32),
            pltpu.VMEM((2, GATE_ROWS, s), F32),
            pltpu.VMEM((s, LANES), BF16),
            pltpu.VMEM((s, LANES), BF16),
            pltpu.VMEM((2, s, LANES), BF16),
            pltpu.VMEM((2, s, LANES), BF16),
            pltpu.VMEM((2, s, 2 * LANES), BF16),
            pltpu.VMEM((2, s, LANES), F32),
            pltpu.VMEM((2, n_chunks, 8, LANES), F32),
            pltpu.VMEM((2, s, LANES), F32),
            pltpu.VMEM((n_dir, n_chunks, 8, LANES), F32),
            pltpu.VMEM((n_dir, n_chunks, HEAD_DIM, LANES), BF16),
            pltpu.VMEM((n_dir, n_chunks, HEAD_DIM, LANES), BF16),
            pltpu.VMEM((n_dir, n_chunks, CHUNK, LANES), BF16),
            pltpu.VMEM((n_dir, n_chunks, CHUNK, LANES), BF16),
            pltpu.VMEM((n_dir, s, LANES), F32),
        ],
        compiler_params=pltpu.CompilerParams(
            dimension_semantics=("arbitrary", "arbitrary"), vmem_limit_bytes=VMEM_LIMIT),
        name="deltanet",
    )(gqkv, gqkv, gqkv, gz, gates, conv_w, conv_w, conv_w, gpar, norm_w2)


def _mix_ffn_kernel(yc_ref, yd_ref, yg_ref, wo_ref, x_ref, nwm_ref, nw1_ref, w1_ref, w2_ref, nw2_ref, o_ref,
                    xm_s, hb_s, acc_s):
    j = pl.program_id(1)

    @pl.when(j == 0)
    def _():
        c0 = yc_ref.shape[-1]
        c1 = c0 + yd_ref.shape[-1]
        y = (_mm(yc_ref[...], wo_ref[0:c0, :]) + _mm(yd_ref[...], wo_ref[c0:c1, :])
             + _mm(yg_ref[...], wo_ref[c1:, :]))
        xm = x_ref[...] + _rms(y, nwm_ref[...], NORM_EPS)
        xm_s[...] = xm
        hb_s[...] = _rms(xm, nw1_ref[...], NORM_EPS).astype(BF16)

    a = _mm(hb_s[...], w1_ref[...])
    a = jnp.square(jnp.maximum(a, 0.0)).astype(BF16)
    part = _mm(a, w2_ref[...])

    @pl.when(j == 0)
    def _():
        acc_s[...] = part

    @pl.when(j > 0)
    def _():
        acc_s[...] = acc_s[...] + part

    @pl.when(j == pl.num_programs(1) - 1)
    def _():
        o_ref[...] = xm_s[...] + _rms(acc_s[...], nw2_ref[...], NORM_EPS)


def _mix_ffn(yc, yd, yg, wo_all, x2, nwm, nw1, w1_all, w2_all, nw2, layer, tm, tf):
    m, d = x2.shape
    dff = w1_all.shape[-1]
    row = lambda a: pl.BlockSpec((tm, a.shape[-1]), lambda i, j: (i, 0))
    vec = lambda: pl.BlockSpec((None, 1, d), lambda i, j: (layer, 0, 0))
    return pl.pallas_call(
        _mix_ffn_kernel,
        grid=(m // tm, dff // tf),
        in_specs=[
            row(yc), row(yd), row(yg),
            pl.BlockSpec((None, d, d), lambda i, j: (layer, 0, 0)),
            pl.BlockSpec((tm, d), lambda i, j: (i, 0)),
            vec(), vec(),
            pl.BlockSpec((None, d, tf), lambda i, j: (layer, 0, j)),
            pl.BlockSpec((None, tf, d), lambda i, j: (layer, j, 0)),
            vec(),
        ],
        out_specs=pl.BlockSpec((tm, d), lambda i, j: (i, 0)),
        out_shape=jax.ShapeDtypeStruct((m, d), F32),
        scratch_shapes=[pltpu.VMEM((tm, d), F32), pltpu.VMEM((tm, d), BF16), pltpu.VMEM((tm, d), F32)],
        compiler_params=pltpu.CompilerParams(
            dimension_semantics=("arbitrary", "arbitrary"), vmem_limit_bytes=VMEM_LIMIT),
        name="mix_ffn",
    )(yc, yd, yg, wo_all, x2, nwm, nw1, w1_all, w2_all, nw2)


def kernel(x, w_in, w_out, pre_mix_w, post_mix_w, pre_mlp_w, post_mlp_w, w_ff1, w_ff2, conv_dw_w, conv_dw_b, conv_ln_w, conv_ln_b, diff_lambda_q1, diff_lambda_k1, diff_lambda_q2, diff_lambda_k2, diff_subln_w, delta_conv_w, delta_A_log, delta_dt_bias, delta_norm_w):
    b, s, d = x.shape
    depth = w_in.shape[0]
    conv_ch = conv_dw_w.shape[-1]
    delta_w = delta_conv_w.shape[-1] // 3
    n_delta_heads = delta_w // HEAD_DIM
    in_w = w_in.shape[-1]
    diff_w = (in_w - 2 * conv_ch - 4 * delta_w - 4 * n_delta_heads) // 3
    n_diff_heads = diff_w // HEAD_DIM
    n_gates = 4 * n_delta_heads
    splits = (2 * conv_ch, 3 * diff_w, 3 * delta_w, delta_w, LANES)
    m = b * s
    tm = min(512, m)

    w_in_b = jnp.pad(w_in, ((0, 0), (0, 0), (0, LANES - n_gates))).astype(BF16)
    w_out_b = w_out.astype(BF16)
    w1_b = w_ff1.astype(BF16)
    w2_b = w_ff2.astype(BF16)
    row3 = lambda a: a.reshape(depth, 1, a.shape[-1])
    lam_all = jnp.stack([diff_lambda_q1, diff_lambda_k1, diff_lambda_q2, diff_lambda_k2], axis=1)
    gate_par = jnp.stack([delta_A_log.reshape(depth, -1), delta_dt_bias.reshape(depth, -1)], axis=1)
    gate_par = jnp.pad(gate_par, ((0, 0), (0, 0), (2 * n_delta_heads, GATE_ROWS - n_gates)))
    gate_par = jnp.broadcast_to(gate_par[..., None], (depth, 2, GATE_ROWS, LANES))
    delta_nw2 = row3(jnp.concatenate([delta_norm_w, delta_norm_w], axis=-1))

    x2 = x.reshape(m, d)
    for l in range(depth):
        uconv, dqkv, gqkv, gz, gates = _in_proj(x2, row3(pre_mix_w), w_in_b, l, splits, tm)
        y_conv = _convmod(uconv.reshape(b, s, -1), conv_dw_w, row3(conv_dw_b), row3(conv_ln_w),
                          row3(conv_ln_b), l)
        y_diff = _diffattn(dqkv.reshape(b, s, -1), lam_all, row3(diff_subln_w), l, n_diff_heads, min(256, s))
        y_delta = _deltanet(gqkv.reshape(b, s, -1), gz.reshape(b, s, -1), gates.reshape(b, s, -1),
                            delta_conv_w, gate_par, delta_nw2, l, n_delta_heads)
        x2 = _mix_ffn(y_conv.reshape(m, -1), y_diff.reshape(m, -1), y_delta.reshape(m, -1), w_out_b, x2,
                      row3(post_mix_w), row3(pre_mlp_w), w1_b, w2_b, row3(post_mlp_w), l, min(1024, m), 1024)
    return x2.reshape(b, s, d)
```

```python
import functools
import math

import jax
import jax.numpy as jnp
from jax import lax
from jax.experimental import pallas as pl
from jax.experimental.pallas import tpu as pltpu

F32 = jnp.float32
BF16 = jnp.bfloat16

HEAD_DIM = 64
DIFF_DIM = 32
CONV_WIDTH = 31
SHORT_CONV = 3
CHUNK = 64
NORM_EPS = 1e-6
LANES = 128
GATE_ROWS = 32
VMEM_LIMIT = 56 * 1024 * 1024


def _alibi_slopes(n):
    def pow2(m):
        start = 2.0 ** (-8.0 / m)
        return [start ** (i + 1) for i in range(m)]
    if math.log2(n).is_integer():
        return pow2(n)
    c = 2 ** int(math.floor(math.log2(n)))
    return pow2(c) + pow2(2 * c)[0::2][: n - c]


def _rms(x, w, eps):
    return x * lax.rsqrt(jnp.mean(x * x, axis=-1, keepdims=True) + eps) * w


def _mm(a, b):
    return jnp.dot(a, b, preferred_element_type=F32)


def _mm_nt(a, b):
    return lax.dot_general(a, b, (((1,), (1,)), ((), ())), preferred_element_type=F32)


def _mm_tn(a, b):
    return lax.dot_general(a, b, (((0,), (0,)), ((), ())), preferred_element_type=F32)


def _mm_exact_rhs(x, w):
    hi = x.astype(BF16)
    r1 = x - hi.astype(F32)
    mid = r1.astype(BF16)
    lo = (r1 - mid.astype(F32)).astype(BF16)
    return _mm(hi, w) + _mm(mid, w) + _mm(lo, w)


def _softplus(x):
    return jnp.maximum(x, 0.0) + jnp.log1p(jnp.exp(-jnp.abs(x)))


def _silu(x):
    h = 0.5 * x
    return h + h * jnp.tanh(h)


def _in_proj_kernel(x_ref, nw_ref, w_ref, uconv_ref, dqkv_ref, gqkv_ref, gz_ref, gate_ref, *, splits):
    hb = _rms(x_ref[...], nw_ref[...], NORM_EPS).astype(BF16)
    outs = (uconv_ref, dqkv_ref, gqkv_ref, gz_ref, gate_ref)
    off = 0
    for o_ref, width in zip(outs, splits):
        o_ref[...] = _mm(hb, w_ref[:, off:off + width]).astype(o_ref.dtype)
        off += width


def _in_proj(x2, nw, w_all, layer, splits, tm):
    m, d = x2.shape
    npad = w_all.shape[-1]
    dts = (BF16, BF16, BF16, BF16, F32)
    return pl.pallas_call(
        functools.partial(_in_proj_kernel, splits=splits),
        grid=(m // tm,),
        in_specs=[
            pl.BlockSpec((tm, d), lambda i: (i, 0)),
            pl.BlockSpec((None, 1, d), lambda i: (layer, 0, 0)),
            pl.BlockSpec((None, d, npad), lambda i: (layer, 0, 0)),
        ],
        out_specs=[pl.BlockSpec((tm, w), lambda i: (i, 0)) for w in splits],
        out_shape=[jax.ShapeDtypeStruct((m, w), dt) for w, dt in zip(splits, dts)],
        compiler_params=pltpu.CompilerParams(dimension_semantics=("arbitrary",), vmem_limit_bytes=VMEM_LIMIT),
        name="in_proj",
    )(x2, nw, w_all)


def _convmod_kernel(u_ref, w_ref, b_ref, lnw_ref, lnb_ref, o_ref, pad_ref, *, seq, ch, tile):
    front = 16
    u = u_ref[...].astype(F32)
    hg = 0.5 * u[:, ch:]
    h = u[:, :ch] * (0.5 + 0.5 * jnp.tanh(hg))
    n_cols = ch // LANES
    tail = pad_ref.shape[1] - front - seq
    for c in range(n_cols):
        pad_ref[c, 0:front, :] = jnp.zeros((front, LANES), F32)
        pad_ref[c, front:front + seq, :] = h[:, c * LANES:(c + 1) * LANES]
        pad_ref[c, front + seq:, :] = jnp.zeros((tail, LANES), F32)
    half = (CONV_WIDTH - 1) // 2

    for t in range(seq // tile):
        base = t * tile
        cols = []
        for c in range(n_cols):
            part = jnp.zeros((tile, LANES), F32)
            for k in range(CONV_WIDTH):
                start = base + k + front - half
                part = part + pad_ref[c, start:start + tile, :] * w_ref[k:k + 1, c * LANES:(c + 1) * LANES]
            cols.append(part)
        acc = jnp.concatenate(cols, axis=1) + b_ref[...]
        mu = jnp.mean(acc, axis=-1, keepdims=True)
        xc = acc - mu
        var = jnp.mean(xc * xc, axis=-1, keepdims=True)
        y = xc * lax.rsqrt(var + 1e-5) * lnw_ref[...] + lnb_ref[...]
        o_ref[base:base + tile, :] = _silu(y).astype(o_ref.dtype)


def _convmod(uconv, dw_w, dw_b, ln_w, ln_b, layer):
    b, s, c2 = uconv.shape
    ch = c2 // 2
    tile = 64
    vec = lambda: pl.BlockSpec((None, 1, ch), lambda i: (layer, 0, 0))
    return pl.pallas_call(
        functools.partial(_convmod_kernel, seq=s, ch=ch, tile=tile),
        grid=(b,),
        in_specs=[
            pl.BlockSpec((None, s, c2), lambda i: (i, 0, 0)),
            pl.BlockSpec((None, CONV_WIDTH, ch), lambda i: (layer, 0, 0)),
            vec(), vec(), vec(),
        ],
        out_specs=pl.BlockSpec((None, s, ch), lambda i: (i, 0, 0)),
        out_shape=jax.ShapeDtypeStruct((b, s, ch), BF16),
        scratch_shapes=[pltpu.VMEM((ch // LANES, s + 40, LANES), F32)],
        compiler_params=pltpu.CompilerParams(dimension_semantics=("arbitrary",), vmem_limit_bytes=VMEM_LIMIT),
        name="convmod",
    )(uconv, dw_w, dw_b, ln_w, ln_b)


def _pos_features(pos, slope):
    return slope * (pos & -CHUNK).astype(F32), slope * (pos & (CHUNK - 1)).astype(F32)


def _diffattn_kernel(lam_ref, q_ref, k_ref, v_ref, sw_ref, o_ref, kx_s, vx_s, *, tq, seq, slopes, lambda_init):
    p = pl.program_id(1)
    qi = pl.program_id(2)
    n_kt = seq // tq
    hd, dd = HEAD_DIM, DIFF_DIM
    n_pairs = len(slopes) // 2
    assert all(math.log2(s).is_integer() for s in slopes)

    def head_slope(hh):
        slope = jnp.float32(slopes[2 * (n_pairs - 1) + hh])
        for pp in range(n_pairs - 2, -1, -1):
            slope = jnp.where(p == pp, jnp.float32(slopes[2 * pp + hh]), slope)
        return slope

    def feature_lanes(j, shape):
        lane = lax.broadcasted_iota(jnp.int32, shape, 1)
        data = (lane >= dd * j) & (lane < dd * (j + 1))
        return data, lane - dd * (1 - j)

    @pl.when(qi == 0)
    def _():
        pos = lax.broadcasted_iota(jnp.int32, (seq, hd), 0)
        fl = lax.broadcasted_iota(jnp.int32, (seq, hd), 1) & (dd - 1)
        for hh in range(2):
            f1, f2 = _pos_features(pos, head_slope(hh))
            kh = k_ref[:, hd * hh:hd * (hh + 1)].astype(F32)
            feat = jnp.where(fl < 2, 1.0, jnp.where(fl == 2, f1, jnp.where(fl == 3, f2, 0.0)))
            for j in range(2):
                data, _ = feature_lanes(j, (seq, hd))
                kx_s[0, 2 * hh + j] = jnp.where(data, kh, feat).astype(BF16)
                kx_s[1, 2 * hh + j] = jnp.where(data, kh, -feat).astype(BF16)
            vx_s[hh] = jnp.concatenate([v_ref[:, hd * hh:hd * (hh + 1)], jnp.ones((seq, hd), BF16)], axis=1)

    lp = lam_ref[...]
    lam = (jnp.exp(jnp.sum(lp[0:1] * lp[1:2], axis=-1, keepdims=True))
           - jnp.exp(jnp.sum(lp[2:3] * lp[3:4], axis=-1, keepdims=True)) + lambda_init)
    scale = dd ** -0.5
    qpos = lax.broadcasted_iota(jnp.int32, (tq, hd), 0) + qi * tq
    rr = lax.broadcasted_iota(jnp.int32, (tq, tq), 0)
    cc = lax.broadcasted_iota(jnp.int32, (tq, tq), 1)
    ahead = jnp.maximum(cc - rr, 0).astype(F32)
    tiles = []
    for rel in range(n_kt):
        kt = qi + rel
        kt = jnp.where(kt >= n_kt, kt - n_kt, kt)
        tiles.append((jnp.where(kt > qi, 1, 0), pl.ds(pl.multiple_of(kt * tq, tq), tq)))

    def scores(hh, j):
        slope = head_slope(hh)
        f1, f2 = _pos_features(qpos, slope)
        qh = q_ref[:, hd * hh:hd * (hh + 1)].astype(F32) * scale
        data, fl = feature_lanes(j, (tq, hd))
        feat = jnp.where(fl == 0, -f1, jnp.where(fl == 1, -f2, jnp.where(fl < 4, 1.0, 0.0)))
        qx = jnp.where(data, qh, feat).astype(BF16)
        s_tiles = [_mm_nt(qx, kx_s[side, 2 * hh + j, rows, :]) for side, rows in tiles]
        s_tiles[0] = s_tiles[0] + ahead * (-2.0 * slope)
        return jnp.concatenate(s_tiles, axis=1)

    maps = [(hh, j) for hh in range(2) for j in range(2)]
    s_next = scores(*maps[0])
    outs = []
    for n, (hh, j) in enumerate(maps):
        s = s_next
        if n + 1 < len(maps):
            s_next = scores(*maps[n + 1])
        m = jnp.max(s, axis=-1, keepdims=True)
        e = jnp.exp((s - m).astype(BF16))
        ov = None
        for r, (side, rows) in enumerate(tiles):
            part = _mm(e[:, r * tq:(r + 1) * tq], vx_s[hh, rows, :])
            ov = part if ov is None else ov + part
        outs.append(ov[:, :hd] / ov[:, hd:hd + 1])
        if j == 1:
            o = outs[-2] - lam * outs[-1]
            o = _rms(o, sw_ref[...], 1e-5) * (1.0 - lambda_init)
            o_ref[:, hd * hh:hd * (hh + 1)] = o.astype(o_ref.dtype)


def _diffattn(dqkv, lam_all, subln_w, layer, n_heads, tq):
    b, s, w3 = dqkv.shape
    n_pairs = n_heads // 2
    lambda_init = 0.8 - 0.6 * math.exp(-0.3 * layer)
    slopes = tuple(_alibi_slopes(n_heads))
    return pl.pallas_call(
        functools.partial(_diffattn_kernel, tq=tq, seq=s, slopes=slopes, lambda_init=lambda_init),
        grid=(b, n_pairs, s // tq),
        in_specs=[
            pl.BlockSpec((None, 4, DIFF_DIM), lambda i, p, q: (layer, 0, 0)),
            pl.BlockSpec((None, tq, LANES), lambda i, p, q: (i, q, p)),
            pl.BlockSpec((None, s, LANES), lambda i, p, q: (i, 0, n_pairs + p)),
            pl.BlockSpec((None, s, LANES), lambda i, p, q: (i, 0, 2 * n_pairs + p)),
            pl.BlockSpec((None, 1, HEAD_DIM), lambda i, p, q: (layer, 0, 0)),
        ],
        out_specs=pl.BlockSpec((None, tq, LANES), lambda i, p, q: (i, q, p)),
        out_shape=jax.ShapeDtypeStruct((b, s, n_pairs * LANES), BF16),
        scratch_shapes=[
            pltpu.VMEM((2, 4, s, HEAD_DIM), BF16),
            pltpu.VMEM((2, s, LANES), BF16),
        ],
        compiler_params=pltpu.CompilerParams(
            dimension_semantics=("arbitrary", "arbitrary", "arbitrary"), vmem_limit_bytes=VMEM_LIMIT),
        name="diffattn",
    )(lam_all, dqkv, dqkv, dqkv, subln_w)


def _head_blocks(y, low):
    zero = jnp.zeros_like(y)
    return jnp.concatenate([jnp.where(low, y, zero), jnp.where(low, zero, y)], axis=0)


def _unit_tri_inverses(mats, ii, jj, low):
    eye = (ii == jj).astype(F32)
    diag16 = (ii // 16) == (jj // 16)
    bds = [jnp.where(diag16, -a, 0.0) for a in mats]
    ps = [eye + x for x in bds]
    qs = [_mm(x.astype(BF16), _head_blocks(x.astype(BF16), low)) for x in bds]
    for _ in range(2):
        pqs = [_mm(q.astype(BF16), jnp.concatenate([_head_blocks(p.astype(BF16), low),
                                                    _head_blocks(q.astype(BF16), low)], axis=1))
               for p, q in zip(ps, qs)]
        ps = [p + pq[:, :LANES] for p, pq in zip(ps, pqs)]
        qs = [pq[:, LANES:] for pq in pqs]
    ts = [p + _mm(q.astype(BF16), _head_blocks(p.astype(BF16), low)) for p, q in zip(ps, qs)]
    join16 = ((ii // 32) == (jj // 32)) & ((ii // 16) != (jj // 16))
    tbs = [t.astype(BF16) for t in ts]
    tls = [_mm(tb, _head_blocks(jnp.where(join16, a, 0.0).astype(BF16), low)).astype(BF16) for tb, a in zip(tbs, mats)]
    ts = [t - _mm(tl, _head_blocks(tb, low)) for t, tl, tb in zip(ts, tls, tbs)]
    join32 = (ii // 32) != (jj // 32)
    return ts, [jnp.where(join32, a, 0.0).astype(BF16) for a in mats]


def _deltanet_kernel(gq_ref, gk_ref, gv_ref, z_ref, gate_ref, cwq_ref, cwk_ref, cwv_ref, gpar_ref, nw_ref,
                     o_ref,
                     pad_s, kbf_s, qbf_s, qgx_s, kdec_s, rhs_s, gc_s, gr_s, be_s, gl_s, cc_s, wp_s, op_s, qp_s, osum_s,
                     *, seq, n_gate_heads, n_pairs, group_chunks):
    p = pl.program_id(1)
    n_chunks = seq // CHUNK
    hd = HEAD_DIM

    il = lax.broadcasted_iota(jnp.int32, (LANES, LANES), 0)
    jl = lax.broadcasted_iota(jnp.int32, (LANES, LANES), 1)
    same_head = ((il // hd) == (jl // hd)).astype(BF16)

    edge = jnp.zeros((8, LANES), F32)
    pad_s[0:8, :] = edge
    pad_s[8 + seq:, :] = edge

    def short_conv(x_ref, w_ref):
        pad_s[8:8 + seq, :] = x_ref[...].astype(F32)
        y = (pad_s[7:7 + seq, :] * w_ref[0:1, :] + pad_s[8:8 + seq, :] * w_ref[1:2, :]
             + pad_s[9:9 + seq, :] * w_ref[2:3, :])
        return _silu(y)

    def l2n(x):
        return x * lax.rsqrt(_mm_exact_rhs(x * x, same_head) + 1e-6)

    q = l2n(short_conv(gq_ref, cwq_ref)) * (hd ** -0.5)
    k = l2n(short_conv(gk_ref, cwk_ref))
    v = short_conv(gv_ref, cwv_ref)
    kbf_s[...] = k.astype(BF16)
    qbf_s[...] = q.astype(BF16)

    nh = n_gate_heads
    raw = gate_ref[...].T[0:GATE_ROWS]
    reps = seq // LANES
    a_log = jnp.tile(gpar_ref[0], (1, reps))
    dt_bias = jnp.tile(gpar_ref[1], (1, reps))
    beta = jax.nn.sigmoid(raw)
    g = -jnp.exp(a_log) * _softplus(raw + dt_bias)
    gate_row = lax.broadcasted_iota(jnp.int32, (GATE_ROWS, seq), 0)
    pos_in_chunk = lax.broadcasted_iota(jnp.int32, (GATE_ROWS, seq), 1) & (CHUNK - 1)
    fwd, rev = g, g
    for sh in (1, 2, 4, 8, 16, 32):
        fwd = fwd + jnp.where(pos_in_chunk >= sh, pltpu.roll(fwd, sh, 1), 0.0)
        rev = rev + jnp.where(pos_in_chunk < CHUNK - sh, pltpu.roll(rev, seq - sh, 1), 0.0)
    gc_r = jnp.where(gate_row < 3 * nh, fwd, rev)

    ir = lax.broadcasted_iota(jnp.int32, (GATE_ROWS, 2 * LANES), 0)
    jr = lax.broadcasted_iota(jnp.int32, (GATE_ROWS, 2 * LANES), 1)

    def expand(x, base, pieces):
        sel = (ir == base + nh * (jr // LANES) + 2 * p + (jr % LANES) // hd).astype(BF16)
        out = None
        for _ in range(pieces):
            xb = x.astype(BF16)
            part = _mm_tn(xb, sel)
            out = part if out is None else out + part
            x = x - xb.astype(F32)
        return out

    be_all = expand(beta, 0, 2)
    gc_all = expand(gc_r, 2 * nh, 3)

    for d in range(2):
        dl = slice(LANES * d, LANES * (d + 1))
        be = be_all[:, dl]
        gc = gc_all[:, dl]
        gc3 = gc.reshape(n_chunks, CHUNK, LANES)
        last = CHUNK - 1 if d == 0 else 0
        total3 = jnp.broadcast_to(gc3[:, last:last + 1, :], gc3.shape)
        eg = jnp.exp(gc)
        kb = k * be
        vb = (v * be).astype(BF16)
        kbe = (kb * eg).astype(BF16)
        gc_s[d] = gc
        be_s[d] = be
        gct = gc.T
        row_a = gct[0:8]
        row_b = gct[hd:hd + 8]
        row_a_r = pltpu.roll(row_a, seq - hd, 1)
        row_b_r = pltpu.roll(row_b, hd, 1)
        low8 = lax.broadcasted_iota(jnp.int32, (8, LANES), 1) < hd
        for c in range(n_chunks):
            col = slice(LANES * (c // 2), LANES * (c // 2 + 1))
            if c % 2 == 0:
                gr_s[d, c] = jnp.where(low8, row_a[:, col], row_b_r[:, col])
            else:
                gr_s[d, c] = jnp.where(low8, row_a_r[:, col], row_b[:, col])
        gl_s[2 * p + d] = jnp.exp(gc3[:, last:last + 1, :] + jnp.zeros((n_chunks, 8, LANES), F32))
        qgx_s[d] = pltpu.roll(q * eg, hd, 1).astype(BF16)
        kdec_s[d] = (k * jnp.exp(total3 - gc3).reshape(seq, LANES)).astype(BF16)
        rhs_s[d, :, 0 * hd:1 * hd] = vb[:, :hd]
        rhs_s[d, :, 1 * hd:2 * hd] = kbe[:, :hd]
        rhs_s[d, :, 2 * hd:3 * hd] = kbe[:, hd:]
        rhs_s[d, :, 3 * hd:4 * hd] = vb[:, hd:]

    ii = lax.broadcasted_iota(jnp.int32, (CHUNK, LANES), 0)
    jj = lax.broadcasted_iota(jnp.int32, (CHUNK, LANES), 1) & (CHUNK - 1)

    low = lax.broadcasted_iota(jnp.int32, (CHUNK, LANES), 1) < hd
    zeros_rhs = jnp.zeros((CHUNK, LANES), BF16)

    def two_blocks(x):
        return jnp.concatenate([jnp.concatenate([x[:, :LANES], zeros_rhs], axis=1),
                                jnp.concatenate([zeros_rhs, x[:, LANES:]], axis=1)], axis=0)

    def prep(g, carry):
        keys, a_list, aqk_list = [], [], []
        for ci in range(group_chunks):
            c = g * group_chunks + ci
            rows = pl.ds(pl.multiple_of(c * CHUNK, CHUNK), CHUNK)
            kpk = kbf_s[rows, :]
            kq = _mm_nt(jnp.concatenate([kpk, qbf_s[rows, :]], axis=0), _head_blocks(kpk, low))
            kk, qk = kq[:CHUNK], kq[CHUNK:]
            for d in range(2):
                diff = gc_s[d, rows, :] - jnp.concatenate([gr_s[d, c]] * (CHUNK // 8), axis=0)
                incl = (ii >= jj) if d == 0 else (ii <= jj)
                strict = (ii > jj) if d == 0 else (ii < jj)
                dec = jnp.exp(jnp.where(incl, diff, -1e30))
                a_list.append(jnp.where(strict, kk * dec, 0.0) * be_s[d, rows, :])
                aqk_list.append((qk * dec).astype(BF16))
                keys.append((c, rows, d))
        ts, joins = _unit_tri_inverses(a_list, ii, jj, low)
        tbs = [t.astype(BF16) for t in ts]
        xs = [_mm(tb, two_blocks(rhs_s[d, rows, :])) for tb, (c, rows, d) in zip(tbs, keys)]
        tls = [_mm(tb, _head_blocks(lj, low)).astype(BF16) for tb, lj in zip(tbs, joins)]
        uws = [(x - _mm(tl, two_blocks(x.astype(BF16)))).astype(BF16)
               for x, tl in zip(xs, tls)]
        for uw, aqk, (c, rows, d) in zip(uws, aqk_list, keys):
            kd = _mm_tn(kdec_s[d, rows, :], uw)
            kd_a, kd_b = kd[:hd, :LANES], kd[hd:, LANES:]
            ao = _mm(aqk, two_blocks(uw))
            ao_a, ao_b = ao[:, :LANES], ao[:, LANES:]
            idx = 2 * p + d
            cc_s[idx, c] = jnp.where(low, kd_a, kd_b).astype(BF16)
            wp_s[idx, c] = jnp.where(low, kd_b, kd_a).astype(BF16)
            op_s[idx, c] = jnp.where(low, ao_a, ao_b).astype(BF16)
            qp_s[idx, c] = (qgx_s[d, rows, :].astype(F32) - jnp.where(low, ao_b, ao_a)).astype(BF16)
        return carry

    lax.fori_loop(0, n_chunks // group_chunks, prep, 0)

    @pl.when(p == n_pairs - 1)
    def _():
        def scan(n, states):
            new = []
            for pp in range(n_pairs):
                for d in range(2):
                    idx = 2 * pp + d
                    c = n if d == 0 else n_chunks - 1 - n
                    rows = pl.ds(pl.multiple_of(c * CHUNK, CHUNK), CHUNK)
                    st = states[idx]
                    anti = jnp.concatenate([jnp.where(low, 0.0, st), jnp.where(low, st, 0.0)], axis=0).astype(BF16)
                    osum_s[idx, rows, :] = _mm(qp_s[idx, c], anti) + op_s[idx, c].astype(F32)
                    gl = jnp.concatenate([gl_s[idx, c]] * (hd // 8), axis=0)
                    new.append(st * gl - _mm(wp_s[idx, c], anti) + cc_s[idx, c].astype(F32))
            return tuple(new)

        zero = jnp.zeros((hd, LANES), F32)
        lax.fori_loop(0, n_chunks, scan, (zero,) * (2 * n_pairs))

        for pp in range(n_pairs):
            o = osum_s[2 * pp] + osum_s[2 * pp + 1]
            ms = _mm_exact_rhs(o * o, same_head) * (1.0 / hd)
            zz = z_ref[:, LANES * pp:LANES * (pp + 1)].astype(F32)
            y = o * lax.rsqrt(ms + NORM_EPS) * nw_ref[...] * _silu(zz)
            o_ref[:, LANES * pp:LANES * (pp + 1)] = y.astype(o_ref.dtype)


def _deltanet(gqkv, gz, gates, conv_w, gpar, norm_w2, layer, n_heads):
    b, s, w3 = gqkv.shape
    n_pairs = n_heads // 2
    n_chunks = s // CHUNK
    n_dir = 2 * n_pairs
    blk = lambda off: pl.BlockSpec((None, s, LANES), lambda i, p: (i, 0, off + p))
    cw = lambda off: pl.BlockSpec((None, SHORT_CONV, LANES), lambda i, p: (layer, 0, off + p))
    return pl.pallas_call(
        functools.partial(_deltanet_kernel, seq=s, n_gate_heads=n_heads, n_pairs=n_pairs,
                          group_chunks=min(8, n_chunks)),
        grid=(b, n_pairs),
        in_specs=[
            blk(0), blk(n_pairs), blk(2 * n_pairs),
            pl.BlockSpec((None, s, n_pairs * LANES), lambda i, p: (i, 0, 0)),
            pl.BlockSpec((None, s, LANES), lambda i, p: (i, 0, 0)),
            cw(0), cw(n_pairs), cw(2 * n_pairs),
            pl.BlockSpec((None, 2, GATE_ROWS, LANES), lambda i, p: (layer, 0, 0, 0)),
            pl.BlockSpec((None, 1, LANES), lambda i, p: (layer, 0, 0)),
        ],
        out_specs=pl.BlockSpec((None, s, n_pairs * LANES), lambda i, p: (i, 0, 0)),
        out_shape=jax.ShapeDtypeStruct((b, s, n_pairs * LANES), BF16),
        scratch_shapes=[
            pltpu.VMEM((s + 16, LANES), F32),
            pltpu.VMEM((s, LANES), BF16),
            pltpu.VMEM((s, LANES), BF16),
            pltpu.VMEM((2, s, LANES), BF16),
            pltpu.VMEM((2, s, LANES), BF16),
            pltpu.VMEM((2, s, 2 * LANES), BF16),
            pltpu.VMEM((2, s, LANES), F32),
            pltpu.VMEM((2, n_chunks, 8, LANES), F32),
            pltpu.VMEM((2, s, LANES), F32),
            pltpu.VMEM((n_dir, n_chunks, 8, LANES), F32),
            pltpu.VMEM((n_dir, n_chunks, HEAD_DIM, LANES), BF16),
            pltpu.VMEM((n_dir, n_chunks, HEAD_DIM, LANES), BF16),
            pltpu.VMEM((n_dir, n_chunks, CHUNK, LANES), BF16),
            pltpu.VMEM((n_dir, n_chunks, CHUNK, LANES), BF16),
            pltpu.VMEM((n_dir, s, LANES), F32),
        ],
        compiler_params=pltpu.CompilerParams(
            dimension_semantics=("arbitrary", "arbitrary"), vmem_limit_bytes=VMEM_LIMIT),
        name="deltanet",
    )(gqkv, gqkv, gqkv, gz, gates, conv_w, conv_w, conv_w, gpar, norm_w2)


def _mix_ffn_kernel(yc_ref, yd_ref, yg_ref, wo_ref, x_ref, nwm_ref, nw1_ref, w1_ref, w2_ref, nw2_ref, o_ref,
                    xm_s, hb_s, acc_s):
    j = pl.program_id(1)

    @pl.when(j == 0)
    def _():
        c0 = yc_ref.shape[-1]
        c1 = c0 + yd_ref.shape[-1]
        y = (_mm(yc_ref[...], wo_ref[0:c0, :]) + _mm(yd_ref[...], wo_ref[c0:c1, :])
             + _mm(yg_ref[...], wo_ref[c1:, :]))
        xm = x_ref[...] + _rms(y, nwm_ref[...], NORM_EPS)
        xm_s[...] = xm
        hb_s[...] = _rms(xm, nw1_ref[...], NORM_EPS).astype(BF16)

    a = _mm(hb_s[...], w1_ref[...])
    a = jnp.square(jnp.maximum(a, 0.0)).astype(BF16)
    part = _mm(a, w2_ref[...])

    @pl.when(j == 0)
    def _():
        acc_s[...] = part

    @pl.when(j > 0)
    def _():
        acc_s[...] = acc_s[...] + part

    @pl.when(j == pl.num_programs(1) - 1)
    def _():
        o_ref[...] = xm_s[...] + _rms(acc_s[...], nw2_ref[...], NORM_EPS)


def _mix_ffn(yc, yd, yg, wo_all, x2, nwm, nw1, w1_all, w2_all, nw2, layer, tm, tf):
    m, d = x2.shape
    dff = w1_all.shape[-1]
    row = lambda a: pl.BlockSpec((tm, a.shape[-1]), lambda i, j: (i, 0))
    vec = lambda: pl.BlockSpec((None, 1, d), lambda i, j: (layer, 0, 0))
    return pl.pallas_call(
        _mix_ffn_kernel,
        grid=(m // tm, dff // tf),
        in_specs=[
            row(yc), row(yd), row(yg),
            pl.BlockSpec((None, d, d), lambda i, j: (layer, 0, 0)),
            pl.BlockSpec((tm, d), lambda i, j: (i, 0)),
            vec(), vec(),
            pl.BlockSpec((None, d, tf), lambda i, j: (layer, 0, j)),
            pl.BlockSpec((None, tf, d), lambda i, j: (layer, j, 0)),
            vec(),
        ],
        out_specs=pl.BlockSpec((tm, d), lambda i, j: (i, 0)),
        out_shape=jax.ShapeDtypeStruct((m, d), F32),
        scratch_shapes=[pltpu.VMEM((tm, d), F32), pltpu.VMEM((tm, d), BF16), pltpu.VMEM((tm, d), F32)],
        compiler_params=pltpu.CompilerParams(
            dimension_semantics=("arbitrary", "arbitrary"), vmem_limit_bytes=VMEM_LIMIT),
        name="mix_ffn",
    )(yc, yd, yg, wo_all, x2, nwm, nw1, w1_all, w2_all, nw2)


def kernel(x, w_in, w_out, pre_mix_w, post_mix_w, pre_mlp_w, post_mlp_w, w_ff1, w_ff2, conv_dw_w, conv_dw_b, conv_ln_w, conv_ln_b, diff_lambda_q1, diff_lambda_k1, diff_lambda_q2, diff_lambda_k2, diff_subln_w, delta_conv_w, delta_A_log, delta_dt_bias, delta_norm_w):
    b, s, d = x.shape
    depth = w_in.shape[0]
    conv_ch = conv_dw_w.shape[-1]
    delta_w = delta_conv_w.shape[-1] // 3
    n_delta_heads = delta_w // HEAD_DIM
    in_w = w_in.shape[-1]
    diff_w = (in_w - 2 * conv_ch - 4 * delta_w - 4 * n_delta_heads) // 3
    n_diff_heads = diff_w // HEAD_DIM
    n_gates = 4 * n_delta_heads
    splits = (2 * conv_ch, 3 * diff_w, 3 * delta_w, delta_w, LANES)
    m = b * s
    tm = min(1024, m)

    w_in_b = jnp.pad(w_in, ((0, 0), (0, 0), (0, LANES - n_gates))).astype(BF16)
    w_out_b = w_out.astype(BF16)
    w1_b = w_ff1.astype(BF16)
    w2_b = w_ff2.astype(BF16)
    row3 = lambda a: a.reshape(depth, 1, a.shape[-1])
    lam_all = jnp.stack([diff_lambda_q1, diff_lambda_k1, diff_lambda_q2, diff_lambda_k2], axis=1)
    gate_par = jnp.stack([delta_A_log.reshape(depth, -1), delta_dt_bias.reshape(depth, -1)], axis=1)
    gate_par = jnp.pad(gate_par, ((0, 0), (0, 0), (2 * n_delta_heads, GATE_ROWS - n_gates)))
    gate_par = jnp.broadcast_to(gate_par[..., None], (depth, 2, GATE_ROWS, LANES))
    delta_nw2 = row3(jnp.concatenate([delta_norm_w, delta_norm_w], axis=-1))

    x2 = x.reshape(m, d)
    for l in range(depth):
        uconv, dqkv, gqkv, gz, gates = _in_proj(x2, row3(pre_mix_w), w_in_b, l, splits, tm)
        y_conv = _convmod(uconv.reshape(b, s, -1), conv_dw_w, row3(conv_dw_b), row3(conv_ln_w),
                          row3(conv_ln_b), l)
        y_diff = _diffattn(dqkv.reshape(b, s, -1), lam_all, row3(diff_subln_w), l, n_diff_heads, min(256, s))
        y_delta = _deltanet(gqkv.reshape(b, s, -1), gz.reshape(b, s, -1), gates.reshape(b, s, -1),
                            delta_conv_w, gate_par, delta_nw2, l, n_delta_heads)
        x2 = _mix_ffn(y_conv.reshape(m, -1), y_diff.reshape(m, -1), y_delta.reshape(m, -1), w_out_b, x2,
                      row3(post_mix_w), row3(pre_mlp_w), w1_b, w2_b, row3(post_mlp_w), l, min(1024, m), 1024)
    return x2.reshape(b, s, d)
```

```python
import functools
import math

import jax
import jax.numpy as jnp
from jax import lax
from jax.experimental import pallas as pl
from jax.experimental.pallas import tpu as pltpu

F32 = jnp.float32
BF16 = jnp.bfloat16

HEAD_DIM = 64
DIFF_DIM = 32
CONV_WIDTH = 31
SHORT_CONV = 3
CHUNK = 64
NORM_EPS = 1e-6
LANES = 128
GATE_ROWS = 32
VMEM_LIMIT = 56 * 1024 * 1024


def _alibi_slopes(n):
    def pow2(m):
        start = 2.0 ** (-8.0 / m)
        return [start ** (i + 1) for i in range(m)]
    if math.log2(n).is_integer():
        return pow2(n)
    c = 2 ** int(math.floor(math.log2(n)))
    return pow2(c) + pow2(2 * c)[0::2][: n - c]


def _rms(x, w, eps):
    return x * lax.rsqrt(jnp.mean(x * x, axis=-1, keepdims=True) + eps) * w


def _mm(a, b):
    return jnp.dot(a, b, preferred_element_type=F32)


def _mm_nt(a, b):
    return lax.dot_general(a, b, (((1,), (1,)), ((), ())), preferred_element_type=F32)


def _mm_tn(a, b):
    return lax.dot_general(a, b, (((0,), (0,)), ((), ())), preferred_element_type=F32)


def _mm_exact_rhs(x, w):
    hi = x.astype(BF16)
    r1 = x - hi.astype(F32)
    mid = r1.astype(BF16)
    lo = (r1 - mid.astype(F32)).astype(BF16)
    return _mm(hi, w) + _mm(mid, w) + _mm(lo, w)


def _softplus(x):
    return jnp.maximum(x, 0.0) + jnp.log1p(jnp.exp(-jnp.abs(x)))


def _silu(x):
    h = 0.5 * x
    return h + h * jnp.tanh(h)


def _in_proj_kernel(x_ref, nw_ref, w_ref, uconv_ref, dqkv_ref, gqkv_ref, gz_ref, gate_ref, *, splits):
    hb = _rms(x_ref[...], nw_ref[...], NORM_EPS).astype(BF16)
    outs = (uconv_ref, dqkv_ref, gqkv_ref, gz_ref, gate_ref)
    off = 0
    for o_ref, width in zip(outs, splits):
        o_ref[...] = _mm(hb, w_ref[:, off:off + width]).astype(o_ref.dtype)
        off += width


def _in_proj(x2, nw, w_all, layer, splits, tm):
    m, d = x2.shape
    npad = w_all.shape[-1]
    dts = (BF16, BF16, BF16, BF16, F32)
    return pl.pallas_call(
        functools.partial(_in_proj_kernel, splits=splits),
        grid=(m // tm,),
        in_specs=[
            pl.BlockSpec((tm, d), lambda i: (i, 0)),
            pl.BlockSpec((None, 1, d), lambda i: (layer, 0, 0)),
            pl.BlockSpec((None, d, npad), lambda i: (layer, 0, 0)),
        ],
        out_specs=[pl.BlockSpec((tm, w), lambda i: (i, 0)) for w in splits],
        out_shape=[jax.ShapeDtypeStruct((m, w), dt) for w, dt in zip(splits, dts)],
        compiler_params=pltpu.CompilerParams(dimension_semantics=("arbitrary",), vmem_limit_bytes=VMEM_LIMIT),
        name="in_proj",
    )(x2, nw, w_all)


def _convmod_kernel(u_ref, w_ref, b_ref, lnw_ref, lnb_ref, o_ref, pad_ref, *, seq, ch, tile):
    front = 16
    u = u_ref[...].astype(F32)
    hg = 0.5 * u[:, ch:]
    h = u[:, :ch] * (0.5 + 0.5 * jnp.tanh(hg))
    n_cols = ch // LANES
    tail = pad_ref.shape[1] - front - seq
    for c in range(n_cols):
        pad_ref[c, 0:front, :] = jnp.zeros((front, LANES), F32)
        pad_ref[c, front:front + seq, :] = h[:, c * LANES:(c + 1) * LANES]
        pad_ref[c, front + seq:, :] = jnp.zeros((tail, LANES), F32)
    half = (CONV_WIDTH - 1) // 2

    for t in range(seq // tile):
        base = t * tile
        cols = []
        for c in range(n_cols):
            part = jnp.zeros((tile, LANES), F32)
            for k in range(CONV_WIDTH):
                start = base + k + front - half
                part = part + pad_ref[c, start:start + tile, :] * w_ref[k:k + 1, c * LANES:(c + 1) * LANES]
            cols.append(part)
        acc = jnp.concatenate(cols, axis=1) + b_ref[...]
        mu = jnp.mean(acc, axis=-1, keepdims=True)
        xc = acc - mu
        var = jnp.mean(xc * xc, axis=-1, keepdims=True)
        y = xc * lax.rsqrt(var + 1e-5) * lnw_ref[...] + lnb_ref[...]
        o_ref[base:base + tile, :] = _silu(y).astype(o_ref.dtype)


def _convmod(uconv, dw_w, dw_b, ln_w, ln_b, layer):
    b, s, c2 = uconv.shape
    ch = c2 // 2
    tile = 64
    vec = lambda: pl.BlockSpec((None, 1, ch), lambda i: (layer, 0, 0))
    return pl.pallas_call(
        functools.partial(_convmod_kernel, seq=s, ch=ch, tile=tile),
        grid=(b,),
        in_specs=[
            pl.BlockSpec((None, s, c2), lambda i: (i, 0, 0)),
            pl.BlockSpec((None, CONV_WIDTH, ch), lambda i: (layer, 0, 0)),
            vec(), vec(), vec(),
        ],
        out_specs=pl.BlockSpec((None, s, ch), lambda i: (i, 0, 0)),
        out_shape=jax.ShapeDtypeStruct((b, s, ch), BF16),
        scratch_shapes=[pltpu.VMEM((ch // LANES, s + 40, LANES), F32)],
        compiler_params=pltpu.CompilerParams(dimension_semantics=("arbitrary",), vmem_limit_bytes=VMEM_LIMIT),
        name="convmod",
    )(uconv, dw_w, dw_b, ln_w, ln_b)


def _pos_features(pos, slope):
    return slope * (pos & -CHUNK).astype(F32), slope * (pos & (CHUNK - 1)).astype(F32)


def _diffattn_kernel(lam_ref, q_ref, k_ref, v_ref, sw_ref, o_ref, kx_s, vx_s, *, tq, seq, slopes, lambda_init):
    p = pl.program_id(1)
    qi = pl.program_id(2)
    n_kt = seq // tq
    hd, dd = HEAD_DIM, DIFF_DIM
    n_pairs = len(slopes) // 2
    assert all(math.log2(s).is_integer() for s in slopes)

    def head_slope(hh):
        slope = jnp.float32(slopes[2 * (n_pairs - 1) + hh])
        for pp in range(n_pairs - 2, -1, -1):
            slope = jnp.where(p == pp, jnp.float32(slopes[2 * pp + hh]), slope)
        return slope

    def feature_lanes(j, shape):
        lane = lax.broadcasted_iota(jnp.int32, shape, 1)
        data = (lane >= dd * j) & (lane < dd * (j + 1))
        return data, lane - dd * (1 - j)

    @pl.when(qi == 0)
    def _():
        pos = lax.broadcasted_iota(jnp.int32, (seq, hd), 0)
        fl = lax.broadcasted_iota(jnp.int32, (seq, hd), 1) & (dd - 1)
        for hh in range(2):
            f1, f2 = _pos_features(pos, head_slope(hh))
            kh = k_ref[:, hd * hh:hd * (hh + 1)].astype(F32)
            feat = jnp.where(fl < 2, 1.0, jnp.where(fl == 2, f1, jnp.where(fl == 3, f2, 0.0)))
            for j in range(2):
                data, _ = feature_lanes(j, (seq, hd))
                kx_s[0, 2 * hh + j] = jnp.where(data, kh, feat).astype(BF16)
                kx_s[1, 2 * hh + j] = jnp.where(data, kh, -feat).astype(BF16)
            vx_s[hh] = jnp.concatenate([v_ref[:, hd * hh:hd * (hh + 1)], jnp.ones((seq, hd), BF16)], axis=1)

    lp = lam_ref[...]
    lam = (jnp.exp(jnp.sum(lp[0:1] * lp[1:2], axis=-1, keepdims=True))
           - jnp.exp(jnp.sum(lp[2:3] * lp[3:4], axis=-1, keepdims=True)) + lambda_init)
    scale = dd ** -0.5
    qpos = lax.broadcasted_iota(jnp.int32, (tq, hd), 0) + qi * tq
    rr = lax.broadcasted_iota(jnp.int32, (tq, tq), 0)
    cc = lax.broadcasted_iota(jnp.int32, (tq, tq), 1)
    ahead = jnp.maximum(cc - rr, 0).astype(F32)
    tiles = []
    for rel in range(n_kt):
        kt = qi + rel
        kt = jnp.where(kt >= n_kt, kt - n_kt, kt)
        tiles.append((jnp.where(kt > qi, 1, 0), pl.ds(pl.multiple_of(kt * tq, tq), tq)))

    def scores(hh, j):
        slope = head_slope(hh)
        f1, f2 = _pos_features(qpos, slope)
        qh = q_ref[:, hd * hh:hd * (hh + 1)].astype(F32) * scale
        data, fl = feature_lanes(j, (tq, hd))
        feat = jnp.where(fl == 0, -f1, jnp.where(fl == 1, -f2, jnp.where(fl < 4, 1.0, 0.0)))
        qx = jnp.where(data, qh, feat).astype(BF16)
        s_tiles = [_mm_nt(qx, kx_s[side, 2 * hh + j, rows, :]) for side, rows in tiles]
        s_tiles[0] = s_tiles[0] + ahead * (-2.0 * slope)
        return jnp.concatenate(s_tiles, axis=1)

    maps = [(hh, j) for hh in range(2) for j in range(2)]
    s_next = scores(*maps[0])
    outs = []
    for n, (hh, j) in enumerate(maps):
        s = s_next
        if n + 1 < len(maps):
            s_next = scores(*maps[n + 1])
        m = jnp.max(s, axis=-1, keepdims=True)
        e = jnp.exp((s - m).astype(BF16))
        ov = None
        for r, (side, rows) in enumerate(tiles):
            part = _mm(e[:, r * tq:(r + 1) * tq], vx_s[hh, rows, :])
            ov = part if ov is None else ov + part
        outs.append(ov[:, :hd] / ov[:, hd:hd + 1])
        if j == 1:
            o = outs[-2] - lam * outs[-1]
            o = _rms(o, sw_ref[...], 1e-5) * (1.0 - lambda_init)
            o_ref[:, hd * hh:hd * (hh + 1)] = o.astype(o_ref.dtype)


def _diffattn(dqkv, lam_all, subln_w, layer, n_heads, tq):
    b, s, w3 = dqkv.shape
    n_pairs = n_heads // 2
    lambda_init = 0.8 - 0.6 * math.exp(-0.3 * layer)
    slopes = tuple(_alibi_slopes(n_heads))
    return pl.pallas_call(
        functools.partial(_diffattn_kernel, tq=tq, seq=s, slopes=slopes, lambda_init=lambda_init),
        grid=(b, n_pairs, s // tq),
        in_specs=[
            pl.BlockSpec((None, 4, DIFF_DIM), lambda i, p, q: (layer, 0, 0)),
            pl.BlockSpec((None, tq, LANES), lambda i, p, q: (i, q, p)),
            pl.BlockSpec((None, s, LANES), lambda i, p, q: (i, 0, n_pairs + p)),
            pl.BlockSpec((None, s, LANES), lambda i, p, q: (i, 0, 2 * n_pairs + p)),
            pl.BlockSpec((None, 1, HEAD_DIM), lambda i, p, q: (layer, 0, 0)),
        ],
        out_specs=pl.BlockSpec((None, tq, LANES), lambda i, p, q: (i, q, p)),
        out_shape=jax.ShapeDtypeStruct((b, s, n_pairs * LANES), BF16),
        scratch_shapes=[
            pltpu.VMEM((2, 4, s, HEAD_DIM), BF16),
            pltpu.VMEM((2, s, LANES), BF16),
        ],
        compiler_params=pltpu.CompilerParams(
            dimension_semantics=("arbitrary", "arbitrary", "arbitrary"), vmem_limit_bytes=VMEM_LIMIT),
        name="diffattn",
    )(lam_all, dqkv, dqkv, dqkv, subln_w)


def _head_blocks(y, low):
    zero = jnp.zeros_like(y)
    return jnp.concatenate([jnp.where(low, y, zero), jnp.where(low, zero, y)], axis=0)


def _unit_tri_inverses(mats, ii, jj, low):
    eye = (ii == jj).astype(F32)
    diag16 = (ii // 16) == (jj // 16)
    bds = [jnp.where(diag16, -a, 0.0) for a in mats]
    ps = [eye + x for x in bds]
    qs = [_mm(x.astype(BF16), _head_blocks(x.astype(BF16), low)) for x in bds]
    for _ in range(2):
        pqs = [_mm(q.astype(BF16), jnp.concatenate([_head_blocks(p.astype(BF16), low),
                                                    _head_blocks(q.astype(BF16), low)], axis=1))
               for p, q in zip(ps, qs)]
        ps = [p + pq[:, :LANES] for p, pq in zip(ps, pqs)]
        qs = [pq[:, LANES:] for pq in pqs]
    ts = [p + _mm(q.astype(BF16), _head_blocks(p.astype(BF16), low)) for p, q in zip(ps, qs)]
    join16 = ((ii // 32) == (jj // 32)) & ((ii // 16) != (jj // 16))
    tbs = [t.astype(BF16) for t in ts]
    tls = [_mm(tb, _head_blocks(jnp.where(join16, a, 0.0).astype(BF16), low)).astype(BF16) for tb, a in zip(tbs, mats)]
    ts = [t - _mm(tl, _head_blocks(tb, low)) for t, tl, tb in zip(ts, tls, tbs)]
    join32 = (ii // 32) != (jj // 32)
    return ts, [jnp.where(join32, a, 0.0).astype(BF16) for a in mats]


def _deltanet_kernel(gq_ref, gk_ref, gv_ref, z_ref, gate_ref, cwq_ref, cwk_ref, cwv_ref, gpar_ref, nw_ref,
                     o_ref,
                     pad_s, kbf_s, qbf_s, qgx_s, kdec_s, rhs_s, gc_s, gr_s, be_s, gl_s, cc_s, wp_s, op_s, qp_s, osum_s,
                     *, seq, n_gate_heads, n_pairs, group_chunks):
    p = pl.program_id(1)
    n_chunks = seq // CHUNK
    hd = HEAD_DIM

    il = lax.broadcasted_iota(jnp.int32, (LANES, LANES), 0)
    jl = lax.broadcasted_iota(jnp.int32, (LANES, LANES), 1)
    same_head = ((il // hd) == (jl // hd)).astype(BF16)

    edge = jnp.zeros((8, LANES), F32)
    pad_s[0:8, :] = edge
    pad_s[8 + seq:, :] = edge

    def short_conv(x_ref, w_ref):
        pad_s[8:8 + seq, :] = x_ref[...].astype(F32)
        y = (pad_s[7:7 + seq, :] * w_ref[0:1, :] + pad_s[8:8 + seq, :] * w_ref[1:2, :]
             + pad_s[9:9 + seq, :] * w_ref[2:3, :])
        return _silu(y)

    def l2n(x):
        xx = x * x
        hi = xx.astype(BF16)
        lo = (xx - hi.astype(F32)).astype(BF16)
        return x * lax.rsqrt(_mm(hi, same_head) + _mm(lo, same_head) + 1e-6)

    q = l2n(short_conv(gq_ref, cwq_ref)) * (hd ** -0.5)
    k = l2n(short_conv(gk_ref, cwk_ref))
    v = short_conv(gv_ref, cwv_ref)
    kbf_s[...] = k.astype(BF16)
    qbf_s[...] = q.astype(BF16)

    nh = n_gate_heads
    raw = gate_ref[...].T[0:GATE_ROWS]
    reps = seq // LANES
    a_log = jnp.tile(gpar_ref[0], (1, reps))
    dt_bias = jnp.tile(gpar_ref[1], (1, reps))
    beta = jax.nn.sigmoid(raw)
    g = -jnp.exp(a_log) * _softplus(raw + dt_bias)
    gate_row = lax.broadcasted_iota(jnp.int32, (GATE_ROWS, seq), 0)
    pos_in_chunk = lax.broadcasted_iota(jnp.int32, (GATE_ROWS, seq), 1) & (CHUNK - 1)
    fwd, rev = g, g
    for sh in (1, 2, 4, 8, 16, 32):
        fwd = fwd + jnp.where(pos_in_chunk >= sh, pltpu.roll(fwd, sh, 1), 0.0)
        rev = rev + jnp.where(pos_in_chunk < CHUNK - sh, pltpu.roll(rev, seq - sh, 1), 0.0)
    gc_r = jnp.where(gate_row < 3 * nh, fwd, rev)

    ir = lax.broadcasted_iota(jnp.int32, (GATE_ROWS, 2 * LANES), 0)
    jr = lax.broadcasted_iota(jnp.int32, (GATE_ROWS, 2 * LANES), 1)

    def expand(x, base, pieces):
        sel = (ir == base + nh * (jr // LANES) + 2 * p + (jr % LANES) // hd).astype(BF16)
        out = None
        for _ in range(pieces):
            xb = x.astype(BF16)
            part = _mm_tn(xb, sel)
            out = part if out is None else out + part
            x = x - xb.astype(F32)
        return out

    be_all = expand(beta, 0, 1)
    gc_all = expand(gc_r, 2 * nh, 3)

    for d in range(2):
        dl = slice(LANES * d, LANES * (d + 1))
        be = be_all[:, dl]
        gc = gc_all[:, dl]
        gc3 = gc.reshape(n_chunks, CHUNK, LANES)
        last = CHUNK - 1 if d == 0 else 0
        total3 = jnp.broadcast_to(gc3[:, last:last + 1, :], gc3.shape)
        eg = jnp.exp(gc)
        kb = k * be
        vb = (v * be).astype(BF16)
        kbe = (kb * eg).astype(BF16)
        gc_s[d] = gc
        be_s[d] = be
        gct = gc.T
        row_a = gct[0:8]
        row_b = gct[hd:hd + 8]
        row_a_r = pltpu.roll(row_a, seq - hd, 1)
        row_b_r = pltpu.roll(row_b, hd, 1)
        low8 = lax.broadcasted_iota(jnp.int32, (8, LANES), 1) < hd
        for c in range(n_chunks):
            col = slice(LANES * (c // 2), LANES * (c // 2 + 1))
            if c % 2 == 0:
                gr_s[d, c] = jnp.where(low8, row_a[:, col], row_b_r[:, col])
            else:
                gr_s[d, c] = jnp.where(low8, row_a_r[:, col], row_b[:, col])
        gl_s[2 * p + d] = jnp.exp(gc3[:, last:last + 1, :] + jnp.zeros((n_chunks, 8, LANES), F32))
        qgx_s[d] = pltpu.roll(q * eg, hd, 1).astype(BF16)
        kdec_s[d] = (k * jnp.exp(total3 - gc3).reshape(seq, LANES)).astype(BF16)
        rhs_s[d, :, 0 * hd:1 * hd] = vb[:, :hd]
        rhs_s[d, :, 1 * hd:2 * hd] = kbe[:, :hd]
        rhs_s[d, :, 2 * hd:3 * hd] = kbe[:, hd:]
        rhs_s[d, :, 3 * hd:4 * hd] = vb[:, hd:]

    ii = lax.broadcasted_iota(jnp.int32, (CHUNK, LANES), 0)
    jj = lax.broadcasted_iota(jnp.int32, (CHUNK, LANES), 1) & (CHUNK - 1)

    low = lax.broadcasted_iota(jnp.int32, (CHUNK, LANES), 1) < hd
    zeros_rhs = jnp.zeros((CHUNK, LANES), BF16)

    def two_blocks(x):
        return jnp.concatenate([jnp.concatenate([x[:, :LANES], zeros_rhs], axis=1),
                                jnp.concatenate([zeros_rhs, x[:, LANES:]], axis=1)], axis=0)

    def prep(g, carry):
        keys, a_list, aqk_list = [], [], []
        for ci in range(group_chunks):
            c = g * group_chunks + ci
            rows = pl.ds(pl.multiple_of(c * CHUNK, CHUNK), CHUNK)
            kpk = kbf_s[rows, :]
            kq = _mm_nt(jnp.concatenate([kpk, qbf_s[rows, :]], axis=0), _head_blocks(kpk, low))
            kk, qk = kq[:CHUNK], kq[CHUNK:]
            for d in range(2):
                diff = gc_s[d, rows, :] - jnp.concatenate([gr_s[d, c]] * (CHUNK // 8), axis=0)
                incl = (ii >= jj) if d == 0 else (ii <= jj)
                strict = (ii > jj) if d == 0 else (ii < jj)
                dec = jnp.exp(jnp.where(incl, diff, -1e30))
                a_list.append(jnp.where(strict, kk * dec, 0.0) * be_s[d, rows, :])
                aqk_list.append((qk * dec).astype(BF16))
                keys.append((c, rows, d))
        ts, joins = _unit_tri_inverses(a_list, ii, jj, low)
        tbs = [t.astype(BF16) for t in ts]
        xs = [_mm(tb, two_blocks(rhs_s[d, rows, :])) for tb, (c, rows, d) in zip(tbs, keys)]
        tls = [_mm(tb, _head_blocks(lj, low)).astype(BF16) for tb, lj in zip(tbs, joins)]
        uws = [(x - _mm(tl, two_blocks(x.astype(BF16)))).astype(BF16)
               for x, tl in zip(xs, tls)]
        for uw, aqk, (c, rows, d) in zip(uws, aqk_list, keys):
            kd = _mm_tn(kdec_s[d, rows, :], uw)
            kd_a, kd_b = kd[:hd, :LANES], kd[hd:, LANES:]
            ao = _mm(aqk, two_blocks(uw))
            ao_a, ao_b = ao[:, :LANES], ao[:, LANES:]
            idx = 2 * p + d
            cc_s[idx, c] = jnp.where(low, kd_a, kd_b).astype(BF16)
            wp_s[idx, c] = jnp.where(low, kd_b, kd_a).astype(BF16)
            op_s[idx, c] = jnp.where(low, ao_a, ao_b).astype(BF16)
            qp_s[idx, c] = (qgx_s[d, rows, :].astype(F32) - jnp.where(low, ao_b, ao_a)).astype(BF16)
        return carry

    lax.fori_loop(0, n_chunks // group_chunks, prep, 0)

    @pl.when(p == n_pairs - 1)
    def _():
        def scan(n, states):
            new = []
            for pp in range(n_pairs):
                for d in range(2):
                    idx = 2 * pp + d
                    c = n if d == 0 else n_chunks - 1 - n
                    rows = pl.ds(pl.multiple_of(c * CHUNK, CHUNK), CHUNK)
                    st = states[idx]
                    anti = jnp.concatenate([jnp.where(low, 0.0, st), jnp.where(low, st, 0.0)], axis=0).astype(BF16)
                    osum_s[idx, rows, :] = _mm(qp_s[idx, c], anti) + op_s[idx, c].astype(F32)
                    gl = jnp.concatenate([gl_s[idx, c]] * (hd // 8), axis=0)
                    new.append(st * gl - _mm(wp_s[idx, c], anti) + cc_s[idx, c].astype(F32))
            return tuple(new)

        zero = jnp.zeros((hd, LANES), F32)
        lax.fori_loop(0, n_chunks, scan, (zero,) * (2 * n_pairs))

        for pp in range(n_pairs):
            o = osum_s[2 * pp] + osum_s[2 * pp + 1]
            ms = _mm_exact_rhs(o * o, same_head) * (1.0 / hd)
            zz = z_ref[:, LANES * pp:LANES * (pp + 1)].astype(F32)
            y = o * lax.rsqrt(ms + NORM_EPS) * nw_ref[...] * _silu(zz)
            o_ref[:, LANES * pp:LANES * (pp + 1)] = y.astype(o_ref.dtype)


def _deltanet(gqkv, gz, gates, conv_w, gpar, norm_w2, layer, n_heads):
    b, s, w3 = gqkv.shape
    n_pairs = n_heads // 2
    n_chunks = s // CHUNK
    n_dir = 2 * n_pairs
    blk = lambda off: pl.BlockSpec((None, s, LANES), lambda i, p: (i, 0, off + p))
    cw = lambda off: pl.BlockSpec((None, SHORT_CONV, LANES), lambda i, p: (layer, 0, off + p))
    return pl.pallas_call(
        functools.partial(_deltanet_kernel, seq=s, n_gate_heads=n_heads, n_pairs=n_pairs,
                          group_chunks=min(16, n_chunks)),
        grid=(b, n_pairs),
        in_specs=[
            blk(0), blk(n_pairs), blk(2 * n_pairs),
            pl.BlockSpec((None, s, n_pairs * LANES), lambda i, p: (i, 0, 0)),
            pl.BlockSpec((None, s, LANES), lambda i, p: (i, 0, 0)),
            cw(0), cw(n_pairs), cw(2 * n_pairs),
            pl.BlockSpec((None, 2, GATE_ROWS, LANES), lambda i, p: (layer, 0, 0, 0)),
            pl.BlockSpec((None, 1, LANES), lambda i, p: (layer, 0, 0)),
        ],
        out_specs=pl.BlockSpec((None, s, n_pairs * LANES), lambda i, p: (i, 0, 0)),
        out_shape=jax.ShapeDtypeStruct((b, s, n_pairs * LANES), BF16),
        scratch_shapes=[
            pltpu.VMEM((s + 16, LANES), F32),
            pltpu.VMEM((s, LANES), BF16),
            pltpu.VMEM((s, LANES), BF16),
            pltpu.VMEM((2, s, LANES), BF16),
            pltpu.VMEM((2, s, LANES), BF16),
            pltpu.VMEM((2, s, 2 * LANES), BF16),
            pltpu.VMEM((2, s, LANES), F32),
            pltpu.VMEM((2, n_chunks, 8, LANES), F32),
            pltpu.VMEM((2, s, LANES), F32),
            pltpu.VMEM((n_dir, n_chunks, 8, LANES), F32),
            pltpu.VMEM((n_dir, n_chunks, HEAD_DIM, LANES), BF16),
            pltpu.VMEM((n_dir, n_chunks, HEAD_DIM, LANES), BF16),
            pltpu.VMEM((n_dir, n_chunks, CHUNK, LANES), BF16),
            pltpu.VMEM((n_dir, n_chunks, CHUNK, LANES), BF16),
            pltpu.VMEM((n_dir, s, LANES), F32),
        ],
        compiler_params=pltpu.CompilerParams(
            dimension_semantics=("arbitrary", "arbitrary"), vmem_limit_bytes=VMEM_LIMIT),
        name="deltanet",
    )(gqkv, gqkv, gqkv, gz, gates, conv_w, conv_w, conv_w, gpar, norm_w2)


def _mix_ffn_kernel(yc_ref, yd_ref, yg_ref, wo_ref, x_ref, nwm_ref, nw1_ref, w1_ref, w2_ref, nw2_ref, o_ref,
                    xm_s, hb_s, acc_s):
    j = pl.program_id(1)

    @pl.when(j == 0)
    def _():
        c0 = yc_ref.shape[-1]
        c1 = c0 + yd_ref.shape[-1]
        y = (_mm(yc_ref[...], wo_ref[0:c0, :]) + _mm(yd_ref[...], wo_ref[c0:c1, :])
             + _mm(yg_ref[...], wo_ref[c1:, :]))
        xm = x_ref[...] + _rms(y, nwm_ref[...], NORM_EPS)
        xm_s[...] = xm
        hb_s[...] = _rms(xm, nw1_ref[...], NORM_EPS).astype(BF16)

    a = _mm(hb_s[...], w1_ref[...])
    a = jnp.square(jnp.maximum(a, 0.0)).astype(BF16)
    part = _mm(a, w2_ref[...])

    @pl.when(j == 0)
    def _():
        acc_s[...] = part

    @pl.when(j > 0)
    def _():
        acc_s[...] = acc_s[...] + part

    @pl.when(j == pl.num_programs(1) - 1)
    def _():
        o_ref[...] = xm_s[...] + _rms(acc_s[...], nw2_ref[...], NORM_EPS)


def _mix_ffn(yc, yd, yg, wo_all, x2, nwm, nw1, w1_all, w2_all, nw2, layer, tm, tf):
    m, d = x2.shape
    dff = w1_all.shape[-1]
    row = lambda a: pl.BlockSpec((tm, a.shape[-1]), lambda i, j: (i, 0))
    vec = lambda: pl.BlockSpec((None, 1, d), lambda i, j: (layer, 0, 0))
    return pl.pallas_call(
        _mix_ffn_kernel,
        grid=(m // tm, dff // tf),
        in_specs=[
            row(yc), row(yd), row(yg),
            pl.BlockSpec((None, d, d), lambda i, j: (layer, 0, 0)),
            pl.BlockSpec((tm, d), lambda i, j: (i, 0)),
            vec(), vec(),
            pl.BlockSpec((None, d, tf), lambda i, j: (layer, 0, j)),
            pl.BlockSpec((None, tf, d), lambda i, j: (layer, j, 0)),
            vec(),
        ],
        out_specs=pl.BlockSpec((tm, d), lambda i, j: (i, 0)),
        out_shape=jax.ShapeDtypeStruct((m, d), F32),
        scratch_shapes=[pltpu.VMEM((tm, d), F32), pltpu.VMEM((tm, d), BF16), pltpu.VMEM((tm, d), F32)],
        compiler_params=pltpu.CompilerParams(
            dimension_semantics=("arbitrary", "arbitrary"), vmem_limit_bytes=VMEM_LIMIT),
        name="mix_ffn",
    )(yc, yd, yg, wo_all, x2, nwm, nw1, w1_all, w2_all, nw2)


def kernel(x, w_in, w_out, pre_mix_w, post_mix_w, pre_mlp_w, post_mlp_w, w_ff1, w_ff2, conv_dw_w, conv_dw_b, conv_ln_w, conv_ln_b, diff_lambda_q1, diff_lambda_k1, diff_lambda_q2, diff_lambda_k2, diff_subln_w, delta_conv_w, delta_A_log, delta_dt_bias, delta_norm_w):
    b, s, d = x.shape
    depth = w_in.shape[0]
    conv_ch = conv_dw_w.shape[-1]
    delta_w = delta_conv_w.shape[-1] // 3
    n_delta_heads = delta_w // HEAD_DIM
    in_w = w_in.shape[-1]
    diff_w = (in_w - 2 * conv_ch - 4 * delta_w - 4 * n_delta_heads) // 3
    n_diff_heads = diff_w // HEAD_DIM
    n_gates = 4 * n_delta_heads
    splits = (2 * conv_ch, 3 * diff_w, 3 * delta_w, delta_w, LANES)
    m = b * s
    tm = min(1024, m)

    w_in_b = jnp.pad(w_in, ((0, 0), (0, 0), (0, LANES - n_gates))).astype(BF16)
    w_out_b = w_out.astype(BF16)
    w1_b = w_ff1.astype(BF16)
    w2_b = w_ff2.astype(BF16)
    row3 = lambda a: a.reshape(depth, 1, a.shape[-1])
    lam_all = jnp.stack([diff_lambda_q1, diff_lambda_k1, diff_lambda_q2, diff_lambda_k2], axis=1)
    gate_par = jnp.stack([delta_A_log.reshape(depth, -1), delta_dt_bias.reshape(depth, -1)], axis=1)
    gate_par = jnp.pad(gate_par, ((0, 0), (0, 0), (2 * n_delta_heads, GATE_ROWS - n_gates)))
    gate_par = jnp.broadcast_to(gate_par[..., None], (depth, 2, GATE_ROWS, LANES))
    delta_nw2 = row3(jnp.concatenate([delta_norm_w, delta_norm_w], axis=-1))

    x2 = x.reshape(m, d)
    for l in range(depth):
        uconv, dqkv, gqkv, gz, gates = _in_proj(x2, row3(pre_mix_w), w_in_b, l, splits, tm)
        y_conv = _convmod(uconv.reshape(b, s, -1), conv_dw_w, row3(conv_dw_b), row3(conv_ln_w),
                          row3(conv_ln_b), l)
        y_diff = _diffattn(dqkv.reshape(b, s, -1), lam_all, row3(diff_subln_w), l, n_diff_heads, min(256, s))
        y_delta = _deltanet(gqkv.reshape(b, s, -1), gz.reshape(b, s, -1), gates.reshape(b, s, -1),
                            delta_conv_w, gate_par, delta_nw2, l, n_delta_heads)
        x2 = _mix_ffn(y_conv.reshape(m, -1), y_diff.reshape(m, -1), y_delta.reshape(m, -1), w_out_b, x2,
                      row3(post_mix_w), row3(pre_mlp_w), w1_b, w2_b, row3(post_mlp_w), l, min(1024, m), 1024)
    return x2.reshape(b, s, d)
```

```python
import functools
import math

import jax
import jax.numpy as jnp
from jax import lax
from jax.experimental import pallas as pl
from jax.experimental.pallas import tpu as pltpu

F32 = jnp.float32
BF16 = jnp.bfloat16

HEAD_DIM = 64
DIFF_DIM = 32
CONV_WIDTH = 31
SHORT_CONV = 3
CHUNK = 64
NORM_EPS = 1e-6
LANES = 128
GATE_ROWS = 32
VMEM_LIMIT = 56 * 1024 * 1024


def _alibi_slopes(n):
    def pow2(m):
        start = 2.0 ** (-8.0 / m)
        return [start ** (i + 1) for i in range(m)]
    if math.log2(n).is_integer():
        return pow2(n)
    c = 2 ** int(math.floor(math.log2(n)))
    return pow2(c) + pow2(2 * c)[0::2][: n - c]


def _rms(x, w, eps):
    return x * lax.rsqrt(jnp.mean(x * x, axis=-1, keepdims=True) + eps) * w


def _mm(a, b):
    return jnp.dot(a, b, preferred_element_type=F32)


def _mm_nt(a, b):
    return lax.dot_general(a, b, (((1,), (1,)), ((), ())), preferred_element_type=F32)


def _mm_tn(a, b):
    return lax.dot_general(a, b, (((0,), (0,)), ((), ())), preferred_element_type=F32)


def _mm_exact_rhs(x, w):
    hi = x.astype(BF16)
    r1 = x - hi.astype(F32)
    mid = r1.astype(BF16)
    lo = (r1 - mid.astype(F32)).astype(BF16)
    return _mm(hi, w) + _mm(mid, w) + _mm(lo, w)


def _softplus(x):
    return jnp.maximum(x, 0.0) + jnp.log1p(jnp.exp(-jnp.abs(x)))


def _silu(x):
    h = 0.5 * x
    return h + h * jnp.tanh(h)


def _in_proj_kernel(x_ref, nw_ref, w_ref, uconv_ref, dqkv_ref, gqkv_ref, gz_ref, gate_ref, *, splits):
    hb = _rms(x_ref[...], nw_ref[...], NORM_EPS).astype(BF16)
    outs = (uconv_ref, dqkv_ref, gqkv_ref, gz_ref, gate_ref)
    off = 0
    for o_ref, width in zip(outs, splits):
        o_ref[...] = _mm(hb, w_ref[:, off:off + width]).astype(o_ref.dtype)
        off += width


def _in_proj(x2, nw, w_all, layer, splits, tm):
    m, d = x2.shape
    npad = w_all.shape[-1]
    dts = (BF16, BF16, BF16, BF16, F32)
    return pl.pallas_call(
        functools.partial(_in_proj_kernel, splits=splits),
        grid=(m // tm,),
        in_specs=[
            pl.BlockSpec((tm, d), lambda i: (i, 0)),
            pl.BlockSpec((None, 1, d), lambda i: (layer, 0, 0)),
            pl.BlockSpec((None, d, npad), lambda i: (layer, 0, 0)),
        ],
        out_specs=[pl.BlockSpec((tm, w), lambda i: (i, 0)) for w in splits],
        out_shape=[jax.ShapeDtypeStruct((m, w), dt) for w, dt in zip(splits, dts)],
        compiler_params=pltpu.CompilerParams(dimension_semantics=("arbitrary",), vmem_limit_bytes=VMEM_LIMIT),
        name="in_proj",
    )(x2, nw, w_all)


def _convmod_kernel(u_ref, w_ref, b_ref, lnw_ref, lnb_ref, o_ref, pad_ref, *, seq, ch, tile):
    front = 16
    u = u_ref[...].astype(F32)
    hg = 0.5 * u[:, ch:]
    h = u[:, :ch] * (0.5 + 0.5 * jnp.tanh(hg))
    n_cols = ch // LANES
    tail = pad_ref.shape[1] - front - seq
    for c in range(n_cols):
        pad_ref[c, 0:front, :] = jnp.zeros((front, LANES), F32)
        pad_ref[c, front:front + seq, :] = h[:, c * LANES:(c + 1) * LANES]
        pad_ref[c, front + seq:, :] = jnp.zeros((tail, LANES), F32)
    half = (CONV_WIDTH - 1) // 2

    for t in range(seq // tile):
        base = t * tile
        cols = []
        for c in range(n_cols):
            part = jnp.zeros((tile, LANES), F32)
            for k in range(CONV_WIDTH):
                start = base + k + front - half
                part = part + pad_ref[c, start:start + tile, :] * w_ref[k:k + 1, c * LANES:(c + 1) * LANES]
            cols.append(part)
        acc = jnp.concatenate(cols, axis=1) + b_ref[...]
        mu = jnp.mean(acc, axis=-1, keepdims=True)
        xc = acc - mu
        var = jnp.mean(xc * xc, axis=-1, keepdims=True)
        y = xc * lax.rsqrt(var + 1e-5) * lnw_ref[...] + lnb_ref[...]
        o_ref[base:base + tile, :] = _silu(y).astype(o_ref.dtype)


def _convmod(uconv, dw_w, dw_b, ln_w, ln_b, layer):
    b, s, c2 = uconv.shape
    ch = c2 // 2
    tile = 64
    vec = lambda: pl.BlockSpec((None, 1, ch), lambda i: (layer, 0, 0))
    return pl.pallas_call(
        functools.partial(_convmod_kernel, seq=s, ch=ch, tile=tile),
        grid=(b,),
        in_specs=[
            pl.BlockSpec((None, s, c2), lambda i: (i, 0, 0)),
            pl.BlockSpec((None, CONV_WIDTH, ch), lambda i: (layer, 0, 0)),
            vec(), vec(), vec(),
        ],
        out_specs=pl.BlockSpec((None, s, ch), lambda i: (i, 0, 0)),
        out_shape=jax.ShapeDtypeStruct((b, s, ch), BF16),
        scratch_shapes=[pltpu.VMEM((ch // LANES, s + 40, LANES), F32)],
        compiler_params=pltpu.CompilerParams(dimension_semantics=("arbitrary",), vmem_limit_bytes=VMEM_LIMIT),
        name="convmod",
    )(uconv, dw_w, dw_b, ln_w, ln_b)


def _pos_features(pos, slope):
    return slope * (pos & -CHUNK).astype(F32), slope * (pos & (CHUNK - 1)).astype(F32)


def _diffattn_kernel(lam_ref, q_ref, k_ref, v_ref, sw_ref, o_ref, kx_s, vx_s, *, tq, seq, slopes, lambda_init):
    p = pl.program_id(1)
    qi = pl.program_id(2)
    n_kt = seq // tq
    hd, dd = HEAD_DIM, DIFF_DIM
    n_pairs = len(slopes) // 2
    assert all(math.log2(s).is_integer() for s in slopes)

    def head_slope(hh):
        slope = jnp.float32(slopes[2 * (n_pairs - 1) + hh])
        for pp in range(n_pairs - 2, -1, -1):
            slope = jnp.where(p == pp, jnp.float32(slopes[2 * pp + hh]), slope)
        return slope

    def feature_lanes(j, shape):
        lane = lax.broadcasted_iota(jnp.int32, shape, 1)
        data = (lane >= dd * j) & (lane < dd * (j + 1))
        return data, lane - dd * (1 - j)

    @pl.when(qi == 0)
    def _():
        pos = lax.broadcasted_iota(jnp.int32, (seq, hd), 0)
        fl = lax.broadcasted_iota(jnp.int32, (seq, hd), 1) & (dd - 1)
        for hh in range(2):
            f1, f2 = _pos_features(pos, head_slope(hh))
            kh = k_ref[:, hd * hh:hd * (hh + 1)].astype(F32)
            feat = jnp.where(fl < 2, 1.0, jnp.where(fl == 2, f1, jnp.where(fl == 3, f2, 0.0)))
            for j in range(2):
                data, _ = feature_lanes(j, (seq, hd))
                kx_s[0, 2 * hh + j] = jnp.where(data, kh, feat).astype(BF16)
                kx_s[1, 2 * hh + j] = jnp.where(data, kh, -feat).astype(BF16)
            vx_s[hh] = jnp.concatenate([v_ref[:, hd * hh:hd * (hh + 1)], jnp.ones((seq, hd), BF16)], axis=1)

    lp = lam_ref[...]
    lam = (jnp.exp(jnp.sum(lp[0:1] * lp[1:2], axis=-1, keepdims=True))
           - jnp.exp(jnp.sum(lp[2:3] * lp[3:4], axis=-1, keepdims=True)) + lambda_init)
    scale = dd ** -0.5
    qpos = lax.broadcasted_iota(jnp.int32, (tq, hd), 0) + qi * tq
    rr = lax.broadcasted_iota(jnp.int32, (tq, tq), 0)
    cc = lax.broadcasted_iota(jnp.int32, (tq, tq), 1)
    ahead = jnp.maximum(cc - rr, 0).astype(F32)
    tiles = []
    for rel in range(n_kt):
        kt = qi + rel
        kt = jnp.where(kt >= n_kt, kt - n_kt, kt)
        tiles.append((jnp.where(kt > qi, 1, 0), pl.ds(pl.multiple_of(kt * tq, tq), tq)))

    def scores(hh, j):
        slope = head_slope(hh)
        f1, f2 = _pos_features(qpos, slope)
        qh = q_ref[:, hd * hh:hd * (hh + 1)].astype(F32) * scale
        data, fl = feature_lanes(j, (tq, hd))
        feat = jnp.where(fl == 0, -f1, jnp.where(fl == 1, -f2, jnp.where(fl < 4, 1.0, 0.0)))
        qx = jnp.where(data, qh, feat).astype(BF16)
        s_tiles = [_mm_nt(qx, kx_s[side, 2 * hh + j, rows, :]) for side, rows in tiles]
        s_tiles[0] = s_tiles[0] + ahead * (-2.0 * slope)
        return jnp.concatenate(s_tiles, axis=1)

    maps = [(hh, j) for hh in range(2) for j in range(2)]
    s_next = scores(*maps[0])
    outs = []
    for n, (hh, j) in enumerate(maps):
        s = s_next
        if n + 1 < len(maps):
            s_next = scores(*maps[n + 1])
        m = jnp.max(s, axis=-1, keepdims=True)
        e = jnp.exp((s - m).astype(BF16))
        ov = None
        for r, (side, rows) in enumerate(tiles):
            part = _mm(e[:, r * tq:(r + 1) * tq], vx_s[hh, rows, :])
            ov = part if ov is None else ov + part
        outs.append(ov[:, :hd] / ov[:, hd:hd + 1])
        if j == 1:
            o = outs[-2] - lam * outs[-1]
            o = _rms(o, sw_ref[...], 1e-5) * (1.0 - lambda_init)
            o_ref[:, hd * hh:hd * (hh + 1)] = o.astype(o_ref.dtype)


def _diffattn(dqkv, lam_all, subln_w, layer, n_heads, tq):
    b, s, w3 = dqkv.shape
    n_pairs = n_heads // 2
    lambda_init = 0.8 - 0.6 * math.exp(-0.3 * layer)
    slopes = tuple(_alibi_slopes(n_heads))
    return pl.pallas_call(
        functools.partial(_diffattn_kernel, tq=tq, seq=s, slopes=slopes, lambda_init=lambda_init),
        grid=(b, n_pairs, s // tq),
        in_specs=[
            pl.BlockSpec((None, 4, DIFF_DIM), lambda i, p, q: (layer, 0, 0)),
            pl.BlockSpec((None, tq, LANES), lambda i, p, q: (i, q, p)),
            pl.BlockSpec((None, s, LANES), lambda i, p, q: (i, 0, n_pairs + p)),
            pl.BlockSpec((None, s, LANES), lambda i, p, q: (i, 0, 2 * n_pairs + p)),
            pl.BlockSpec((None, 1, HEAD_DIM), lambda i, p, q: (layer, 0, 0)),
        ],
        out_specs=pl.BlockSpec((None, tq, LANES), lambda i, p, q: (i, q, p)),
        out_shape=jax.ShapeDtypeStruct((b, s, n_pairs * LANES), BF16),
        scratch_shapes=[
            pltpu.VMEM((2, 4, s, HEAD_DIM), BF16),
            pltpu.VMEM((2, s, LANES), BF16),
        ],
        compiler_params=pltpu.CompilerParams(
            dimension_semantics=("arbitrary", "arbitrary", "arbitrary"), vmem_limit_bytes=VMEM_LIMIT),
        name="diffattn",
    )(lam_all, dqkv, dqkv, dqkv, subln_w)


def _head_blocks(y, low):
    zero = jnp.zeros_like(y)
    return jnp.concatenate([jnp.where(low, y, zero), jnp.where(low, zero, y)], axis=0)


def _unit_tri_inverses(mats, ii, jj, low):
    eye = (ii == jj).astype(F32)
    diag16 = (ii // 16) == (jj // 16)
    bds = [jnp.where(diag16, -a, 0.0) for a in mats]
    ps = [eye + x for x in bds]
    qs = [_mm(x.astype(BF16), _head_blocks(x.astype(BF16), low)) for x in bds]
    for _ in range(2):
        pqs = [_mm(q.astype(BF16), jnp.concatenate([_head_blocks(p.astype(BF16), low),
                                                    _head_blocks(q.astype(BF16), low)], axis=1))
               for p, q in zip(ps, qs)]
        ps = [p + pq[:, :LANES] for p, pq in zip(ps, pqs)]
        qs = [pq[:, LANES:] for pq in pqs]
    ts = [p + _mm(q.astype(BF16), _head_blocks(p.astype(BF16), low)) for p, q in zip(ps, qs)]
    join16 = ((ii // 32) == (jj // 32)) & ((ii // 16) != (jj // 16))
    tbs = [t.astype(BF16) for t in ts]
    tls = [_mm(tb, _head_blocks(jnp.where(join16, a, 0.0).astype(BF16), low)).astype(BF16) for tb, a in zip(tbs, mats)]
    ts = [t - _mm(tl, _head_blocks(tb, low)) for t, tl, tb in zip(ts, tls, tbs)]
    join32 = (ii // 32) != (jj // 32)
    return ts, [jnp.where(join32, a, 0.0).astype(BF16) for a in mats]


def _deltanet_kernel(gq_ref, gk_ref, gv_ref, z_ref, gate_ref, cwq_ref, cwk_ref, cwv_ref, gpar_ref, nw_ref,
                     o_ref,
                     pad_s, kbf_s, qbf_s, qgx_s, kdec_s, rhs_s, gc_s, gr_s, be_s, gl_s, cc_s, wp_s, op_s, qp_s, osum_s,
                     *, seq, n_gate_heads, n_pairs, group_chunks):
    p = pl.program_id(1)
    n_chunks = seq // CHUNK
    hd = HEAD_DIM

    il = lax.broadcasted_iota(jnp.int32, (LANES, LANES), 0)
    jl = lax.broadcasted_iota(jnp.int32, (LANES, LANES), 1)
    same_head = ((il // hd) == (jl // hd)).astype(BF16)

    edge = jnp.zeros((8, LANES), F32)
    pad_s[0:8, :] = edge
    pad_s[8 + seq:, :] = edge

    def short_conv(x_ref, w_ref):
        pad_s[8:8 + seq, :] = x_ref[...].astype(F32)
        y = (pad_s[7:7 + seq, :] * w_ref[0:1, :] + pad_s[8:8 + seq, :] * w_ref[1:2, :]
             + pad_s[9:9 + seq, :] * w_ref[2:3, :])
        return _silu(y)

    def l2n(x):
        xx = x * x
        hi = xx.astype(BF16)
        lo = (xx - hi.astype(F32)).astype(BF16)
        return x * lax.rsqrt(_mm(hi, same_head) + _mm(lo, same_head) + 1e-6)

    q = l2n(short_conv(gq_ref, cwq_ref)) * (hd ** -0.5)
    k = l2n(short_conv(gk_ref, cwk_ref))
    v = short_conv(gv_ref, cwv_ref)
    kbf_s[...] = k.astype(BF16)
    qbf_s[...] = q.astype(BF16)

    nh = n_gate_heads
    raw = gate_ref[...].T[0:GATE_ROWS]
    reps = seq // LANES
    a_log = jnp.tile(gpar_ref[0], (1, reps))
    dt_bias = jnp.tile(gpar_ref[1], (1, reps))
    beta = jax.nn.sigmoid(raw)
    g = -jnp.exp(a_log) * _softplus(raw + dt_bias)
    gate_row = lax.broadcasted_iota(jnp.int32, (GATE_ROWS, seq), 0)
    pos_in_chunk = lax.broadcasted_iota(jnp.int32, (GATE_ROWS, seq), 1) & (CHUNK - 1)
    fwd, rev = g, g
    for sh in (1, 2, 4, 8, 16, 32):
        fwd = fwd + jnp.where(pos_in_chunk >= sh, pltpu.roll(fwd, sh, 1), 0.0)
        rev = rev + jnp.where(pos_in_chunk < CHUNK - sh, pltpu.roll(rev, seq - sh, 1), 0.0)
    gc_r = jnp.where(gate_row < 3 * nh, fwd, rev)

    ir = lax.broadcasted_iota(jnp.int32, (GATE_ROWS, 2 * LANES), 0)
    jr = lax.broadcasted_iota(jnp.int32, (GATE_ROWS, 2 * LANES), 1)

    def expand(x, base, pieces):
        sel = (ir == base + nh * (jr // LANES) + 2 * p + (jr % LANES) // hd).astype(BF16)
        out = None
        for _ in range(pieces):
            xb = x.astype(BF16)
            part = _mm_tn(xb, sel)
            out = part if out is None else out + part
            x = x - xb.astype(F32)
        return out

    be_all = expand(beta, 0, 1)
    gc_all = expand(gc_r, 2 * nh, 3)

    for d in range(2):
        dl = slice(LANES * d, LANES * (d + 1))
        be = be_all[:, dl]
        gc = gc_all[:, dl]
        gc3 = gc.reshape(n_chunks, CHUNK, LANES)
        last = CHUNK - 1 if d == 0 else 0
        total3 = jnp.broadcast_to(gc3[:, last:last + 1, :], gc3.shape)
        eg = jnp.exp(gc)
        kb = k * be
        vb = (v * be).astype(BF16)
        kbe = (kb * eg).astype(BF16)
        gc_s[d] = gc
        be_s[d] = be
        gct = gc.T
        row_a = gct[0:8]
        row_b = gct[hd:hd + 8]
        row_a_r = pltpu.roll(row_a, seq - hd, 1)
        row_b_r = pltpu.roll(row_b, hd, 1)
        low8 = lax.broadcasted_iota(jnp.int32, (8, LANES), 1) < hd
        for c in range(n_chunks):
            col = slice(LANES * (c // 2), LANES * (c // 2 + 1))
            if c % 2 == 0:
                gr_s[d, c] = jnp.where(low8, row_a[:, col], row_b_r[:, col])
            else:
                gr_s[d, c] = jnp.where(low8, row_a_r[:, col], row_b[:, col])
        gl_s[2 * p + d] = jnp.exp(gc3[:, last:last + 1, :] + jnp.zeros((n_chunks, 8, LANES), F32))
        qgx_s[d] = pltpu.roll(q * eg, hd, 1).astype(BF16)
        kdec_s[d] = (k * jnp.exp(total3 - gc3).reshape(seq, LANES)).astype(BF16)
        rhs_s[d, :, 0 * hd:1 * hd] = vb[:, :hd]
        rhs_s[d, :, 1 * hd:2 * hd] = kbe[:, :hd]
        rhs_s[d, :, 2 * hd:3 * hd] = kbe[:, hd:]
        rhs_s[d, :, 3 * hd:4 * hd] = vb[:, hd:]

    ii = lax.broadcasted_iota(jnp.int32, (CHUNK, LANES), 0)
    jj = lax.broadcasted_iota(jnp.int32, (CHUNK, LANES), 1) & (CHUNK - 1)

    low = lax.broadcasted_iota(jnp.int32, (CHUNK, LANES), 1) < hd
    zeros_rhs = jnp.zeros((CHUNK, LANES), BF16)

    def two_blocks(x):
        return jnp.concatenate([jnp.concatenate([x[:, :LANES], zeros_rhs], axis=1),
                                jnp.concatenate([zeros_rhs, x[:, LANES:]], axis=1)], axis=0)

    def prep(g, carry):
        keys, a_list, aqk_list = [], [], []
        for ci in range(group_chunks):
            c = g * group_chunks + ci
            rows = pl.ds(pl.multiple_of(c * CHUNK, CHUNK), CHUNK)
            kpk = kbf_s[rows, :]
            kq = _mm_nt(jnp.concatenate([kpk, qbf_s[rows, :]], axis=0), _head_blocks(kpk, low))
            kk, qk = kq[:CHUNK], kq[CHUNK:]
            for d in range(2):
                diff = gc_s[d, rows, :] - jnp.concatenate([gr_s[d, c]] * (CHUNK // 8), axis=0)
                incl = (ii >= jj) if d == 0 else (ii <= jj)
                strict = (ii > jj) if d == 0 else (ii < jj)
                dec = jnp.exp(jnp.where(incl, diff, -1e30))
                a_list.append(jnp.where(strict, kk * dec, 0.0) * be_s[d, rows, :])
                aqk_list.append((qk * dec).astype(BF16))
                keys.append((c, rows, d))
        ts, joins = _unit_tri_inverses(a_list, ii, jj, low)
        tbs = [t.astype(BF16) for t in ts]
        xs = [_mm(tb, two_blocks(rhs_s[d, rows, :])) for tb, (c, rows, d) in zip(tbs, keys)]
        tls = [_mm(tb, _head_blocks(lj, low)).astype(BF16) for tb, lj in zip(tbs, joins)]
        uws = [(x - _mm(tl, two_blocks(x.astype(BF16)))).astype(BF16)
               for x, tl in zip(xs, tls)]
        for uw, aqk, (c, rows, d) in zip(uws, aqk_list, keys):
            kd = _mm_tn(kdec_s[d, rows, :], uw)
            kd_a, kd_b = kd[:hd, :LANES], kd[hd:, LANES:]
            ao = _mm(aqk, two_blocks(uw))
            ao_a, ao_b = ao[:, :LANES], ao[:, LANES:]
            idx = 2 * p + d
            cc_s[idx, c] = jnp.where(low, kd_a, kd_b).astype(BF16)
            wp_s[idx, c] = jnp.where(low, kd_b, kd_a).astype(BF16)
            op_s[idx, c] = jnp.where(low, ao_a, ao_b).astype(BF16)
            qp_s[idx, c] = (qgx_s[d, rows, :].astype(F32) - jnp.where(low, ao_b, ao_a)).astype(BF16)
        return carry

    lax.fori_loop(0, n_chunks // group_chunks, prep, 0)

    @pl.when(p == n_pairs - 1)
    def _():
        def scan(n, states):
            new = []
            for pp in range(n_pairs):
                for d in range(2):
                    idx = 2 * pp + d
                    c = n if d == 0 else n_chunks - 1 - n
                    rows = pl.ds(pl.multiple_of(c * CHUNK, CHUNK), CHUNK)
                    st = states[idx]
                    anti = jnp.concatenate([jnp.where(low, 0.0, st), jnp.where(low, st, 0.0)], axis=0).astype(BF16)
                    osum_s[idx, rows, :] = _mm(qp_s[idx, c], anti) + op_s[idx, c].astype(F32)
                    gl = jnp.concatenate([gl_s[idx, c]] * (hd // 8), axis=0)
                    new.append(st * gl - _mm(wp_s[idx, c], anti) + cc_s[idx, c].astype(F32))
            return tuple(new)

        zero = jnp.zeros((hd, LANES), F32)
        lax.fori_loop(0, n_chunks, scan, (zero,) * (2 * n_pairs))

        for pp in range(n_pairs):
            o = osum_s[2 * pp] + osum_s[2 * pp + 1]
            ms = _mm_exact_rhs(o * o, same_head) * (1.0 / hd)
            zz = z_ref[:, LANES * pp:LANES * (pp + 1)].astype(F32)
            y = o * lax.rsqrt(ms + NORM_EPS) * nw_ref[...] * _silu(zz)
            o_ref[:, LANES * pp:LANES * (pp + 1)] = y.astype(o_ref.dtype)


def _deltanet(gqkv, gz, gates, conv_w, gpar, norm_w2, layer, n_heads):
    b, s, w3 = gqkv.shape
    n_pairs = n_heads // 2
    n_chunks = s // CHUNK
    n_dir = 2 * n_pairs
    blk = lambda off: pl.BlockSpec((None, s, LANES), lambda i, p: (i, 0, off + p))
    cw = lambda off: pl.BlockSpec((None, SHORT_CONV, LANES), lambda i, p: (layer, 0, off + p))
    return pl.pallas_call(
        functools.partial(_deltanet_kernel, seq=s, n_gate_heads=n_heads, n_pairs=n_pairs,
                          group_chunks=min(16, n_chunks)),
        grid=(b, n_pairs),
        in_specs=[
            blk(0), blk(n_pairs), blk(2 * n_pairs),
            pl.BlockSpec((None, s, n_pairs * LANES), lambda i, p: (i, 0, 0)),
            pl.BlockSpec((None, s, LANES), lambda i, p: (i, 0, 0)),
            cw(0), cw(n_pairs), cw(2 * n_pairs),
            pl.BlockSpec((None, 2, GATE_ROWS, LANES), lambda i, p: (layer, 0, 0, 0)),
            pl.BlockSpec((None, 1, LANES), lambda i, p: (layer, 0, 0)),
        ],
        out_specs=pl.BlockSpec((None, s, n_pairs * LANES), lambda i, p: (i, 0, 0)),
        out_shape=jax.ShapeDtypeStruct((b, s, n_pairs * LANES), BF16),
        scratch_shapes=[
            pltpu.VMEM((s + 16, LANES), F32),
            pltpu.VMEM((s, LANES), BF16),
            pltpu.VMEM((s, LANES), BF16),
            pltpu.VMEM((2, s, LANES), BF16),
            pltpu.VMEM((2, s, LANES), BF16),
            pltpu.VMEM((2, s, 2 * LANES), BF16),
            pltpu.VMEM((2, s, LANES), F32),
            pltpu.VMEM((2, n_chunks, 8, LANES), F32),
            pltpu.VMEM((2, s, LANES), F32),
            pltpu.VMEM((n_dir, n_chunks, 8, LANES), F32),
            pltpu.VMEM((n_dir, n_chunks, HEAD_DIM, LANES), BF16),
            pltpu.VMEM((n_dir, n_chunks, HEAD_DIM, LANES), BF16),
            pltpu.VMEM((n_dir, n_chunks, CHUNK, LANES), BF16),
            pltpu.VMEM((n_dir, n_chunks, CHUNK, LANES), BF16),
            pltpu.VMEM((n_dir, s, LANES), F32),
        ],
        compiler_params=pltpu.CompilerParams(
            dimension_semantics=("arbitrary", "arbitrary"), vmem_limit_bytes=VMEM_LIMIT),
        name="deltanet",
    )(gqkv, gqkv, gqkv, gz, gates, conv_w, conv_w, conv_w, gpar, norm_w2)


def _mix_ffn_kernel(yc_ref, yd_ref, yg_ref, wo_ref, x_ref, nwm_ref, nw1_ref, w1_ref, w2_ref, nw2_ref, o_ref,
                    xm_s, hb_s, acc_s):
    j = pl.program_id(1)

    @pl.when(j == 0)
    def _():
        c0 = yc_ref.shape[-1]
        c1 = c0 + yd_ref.shape[-1]
        y = (_mm(yc_ref[...], wo_ref[0:c0, :]) + _mm(yd_ref[...], wo_ref[c0:c1, :])
             + _mm(yg_ref[...], wo_ref[c1:, :]))
        xm = x_ref[...] + _rms(y, nwm_ref[...], NORM_EPS)
        xm_s[...] = xm
        hb_s[...] = _rms(xm, nw1_ref[...], NORM_EPS).astype(BF16)
        acc_s[...] = jnp.zeros_like(acc_s)

    a = _mm(hb_s[...], w1_ref[...])
    a = jnp.square(jnp.maximum(a, 0.0)).astype(BF16)
    acc_s[...] += _mm(a, w2_ref[...])

    @pl.when(j == pl.num_programs(1) - 1)
    def _():
        o_ref[...] = xm_s[...] + _rms(acc_s[...], nw2_ref[...], NORM_EPS)


def _mix_ffn(yc, yd, yg, wo_all, x2, nwm, nw1, w1_all, w2_all, nw2, layer, tm, tf):
    m, d = x2.shape
    dff = w1_all.shape[-1]
    row = lambda a: pl.BlockSpec((tm, a.shape[-1]), lambda i, j: (i, 0))
    vec = lambda: pl.BlockSpec((None, 1, d), lambda i, j: (layer, 0, 0))
    return pl.pallas_call(
        _mix_ffn_kernel,
        grid=(m // tm, dff // tf),
        in_specs=[
            row(yc), row(yd), row(yg),
            pl.BlockSpec((None, d, d), lambda i, j: (layer, 0, 0)),
            pl.BlockSpec((tm, d), lambda i, j: (i, 0)),
            vec(), vec(),
            pl.BlockSpec((None, d, tf), lambda i, j: (layer, 0, j)),
            pl.BlockSpec((None, tf, d), lambda i, j: (layer, j, 0)),
            vec(),
        ],
        out_specs=pl.BlockSpec((tm, d), lambda i, j: (i, 0)),
        out_shape=jax.ShapeDtypeStruct((m, d), F32),
        scratch_shapes=[pltpu.VMEM((tm, d), F32), pltpu.VMEM((tm, d), BF16), pltpu.VMEM((tm, d), F32)],
        compiler_params=pltpu.CompilerParams(
            dimension_semantics=("arbitrary", "arbitrary"), vmem_limit_bytes=VMEM_LIMIT),
        name="mix_ffn",
    )(yc, yd, yg, wo_all, x2, nwm, nw1, w1_all, w2_all, nw2)


def kernel(x, w_in, w_out, pre_mix_w, post_mix_w, pre_mlp_w, post_mlp_w, w_ff1, w_ff2, conv_dw_w, conv_dw_b, conv_ln_w, conv_ln_b, diff_lambda_q1, diff_lambda_k1, diff_lambda_q2, diff_lambda_k2, diff_subln_w, delta_conv_w, delta_A_log, delta_dt_bias, delta_norm_w):
    b, s, d = x.shape
    depth = w_in.shape[0]
    conv_ch = conv_dw_w.shape[-1]
    delta_w = delta_conv_w.shape[-1] // 3
    n_delta_heads = delta_w // HEAD_DIM
    in_w = w_in.shape[-1]
    diff_w = (in_w - 2 * conv_ch - 4 * delta_w - 4 * n_delta_heads) // 3
    n_diff_heads = diff_w // HEAD_DIM
    n_gates = 4 * n_delta_heads
    splits = (2 * conv_ch, 3 * diff_w, 3 * delta_w, delta_w, LANES)
    m = b * s
    tm = min(1024, m)

    w_in_b = jnp.pad(w_in, ((0, 0), (0, 0), (0, LANES - n_gates))).astype(BF16)
    w_out_b = w_out.astype(BF16)
    w1_b = w_ff1.astype(BF16)
    w2_b = w_ff2.astype(BF16)
    row3 = lambda a: a.reshape(depth, 1, a.shape[-1])
    lam_all = jnp.stack([diff_lambda_q1, diff_lambda_k1, diff_lambda_q2, diff_lambda_k2], axis=1)
    gate_par = jnp.stack([delta_A_log.reshape(depth, -1), delta_dt_bias.reshape(depth, -1)], axis=1)
    gate_par = jnp.pad(gate_par, ((0, 0), (0, 0), (2 * n_delta_heads, GATE_ROWS - n_gates)))
    gate_par = jnp.broadcast_to(gate_par[..., None], (depth, 2, GATE_ROWS, LANES))
    delta_nw2 = row3(jnp.concatenate([delta_norm_w, delta_norm_w], axis=-1))

    x2 = x.reshape(m, d)
    for l in range(depth):
        uconv, dqkv, gqkv, gz, gates = _in_proj(x2, row3(pre_mix_w), w_in_b, l, splits, tm)
        y_conv = _convmod(uconv.reshape(b, s, -1), conv_dw_w, row3(conv_dw_b), row3(conv_ln_w),
                          row3(conv_ln_b), l)
        y_diff = _diffattn(dqkv.reshape(b, s, -1), lam_all, row3(diff_subln_w), l, n_diff_heads, min(256, s))
        y_delta = _deltanet(gqkv.reshape(b, s, -1), gz.reshape(b, s, -1), gates.reshape(b, s, -1),
                            delta_conv_w, gate_par, delta_nw2, l, n_delta_heads)
        x2 = _mix_ffn(y_conv.reshape(m, -1), y_diff.reshape(m, -1), y_delta.reshape(m, -1), w_out_b, x2,
                      row3(post_mix_w), row3(pre_mlp_w), w1_b, w2_b, row3(post_mlp_w), l, min(1024, m), 1024)
    return x2.reshape(b, s, d)
```

```python
import functools
import math

import jax
import jax.numpy as jnp
from jax import lax
from jax.experimental import pallas as pl
from jax.experimental.pallas import tpu as pltpu

F32 = jnp.float32
BF16 = jnp.bfloat16

HEAD_DIM = 64
DIFF_DIM = 32
CONV_WIDTH = 31
SHORT_CONV = 3
CHUNK = 64
NORM_EPS = 1e-6
LANES = 128
GATE_ROWS = 32
VMEM_LIMIT = 56 * 1024 * 1024


def _alibi_slopes(n):
    def pow2(m):
        start = 2.0 ** (-8.0 / m)
        return [start ** (i + 1) for i in range(m)]
    if math.log2(n).is_integer():
        return pow2(n)
    c = 2 ** int(math.floor(math.log2(n)))
    return pow2(c) + pow2(2 * c)[0::2][: n - c]


def _rms(x, w, eps):
    return x * lax.rsqrt(jnp.mean(x * x, axis=-1, keepdims=True) + eps) * w


def _mm(a, b):
    return jnp.dot(a, b, preferred_element_type=F32)


def _mm_nt(a, b):
    return lax.dot_general(a, b, (((1,), (1,)), ((), ())), preferred_element_type=F32)


def _mm_tn(a, b):
    return lax.dot_general(a, b, (((0,), (0,)), ((), ())), preferred_element_type=F32)


def _mm_exact_rhs(x, w):
    hi = x.astype(BF16)
    r1 = x - hi.astype(F32)
    mid = r1.astype(BF16)
    lo = (r1 - mid.astype(F32)).astype(BF16)
    return _mm(hi, w) + _mm(mid, w) + _mm(lo, w)


def _softplus(x):
    return jnp.maximum(x, 0.0) + jnp.log1p(jnp.exp(-jnp.abs(x)))


def _silu(x):
    h = 0.5 * x
    return h + h * jnp.tanh(h)


def _in_proj_kernel(x_ref, nw_ref, w_ref, uconv_ref, dqkv_ref, gqkv_ref, gz_ref, gate_ref, *, splits):
    hb = _rms(x_ref[...], nw_ref[...], NORM_EPS).astype(BF16)
    outs = (uconv_ref, dqkv_ref, gqkv_ref, gz_ref, gate_ref)
    off = 0
    for o_ref, width in zip(outs, splits):
        o_ref[...] = _mm(hb, w_ref[:, off:off + width]).astype(o_ref.dtype)
        off += width


def _in_proj(x2, nw, w_all, layer, splits, tm):
    m, d = x2.shape
    npad = w_all.shape[-1]
    dts = (BF16, BF16, BF16, BF16, F32)
    return pl.pallas_call(
        functools.partial(_in_proj_kernel, splits=splits),
        grid=(m // tm,),
        in_specs=[
            pl.BlockSpec((tm, d), lambda i: (i, 0)),
            pl.BlockSpec((None, 1, d), lambda i: (layer, 0, 0)),
            pl.BlockSpec((None, d, npad), lambda i: (layer, 0, 0)),
        ],
        out_specs=[pl.BlockSpec((tm, w), lambda i: (i, 0)) for w in splits],
        out_shape=[jax.ShapeDtypeStruct((m, w), dt) for w, dt in zip(splits, dts)],
        compiler_params=pltpu.CompilerParams(dimension_semantics=("arbitrary",), vmem_limit_bytes=VMEM_LIMIT),
        name="in_proj",
    )(x2, nw, w_all)


def _convmod_kernel(u_ref, w_ref, b_ref, lnw_ref, lnb_ref, o_ref, pad_ref, *, seq, ch, tile):
    front = 16
    u = u_ref[...].astype(F32)
    hg = 0.5 * u[:, ch:]
    h = u[:, :ch] * (0.5 + 0.5 * jnp.tanh(hg))
    n_cols = ch // LANES
    tail = pad_ref.shape[1] - front - seq
    for c in range(n_cols):
        pad_ref[c, 0:front, :] = jnp.zeros((front, LANES), F32)
        pad_ref[c, front:front + seq, :] = h[:, c * LANES:(c + 1) * LANES]
        pad_ref[c, front + seq:, :] = jnp.zeros((tail, LANES), F32)
    half = (CONV_WIDTH - 1) // 2

    for t in range(seq // tile):
        base = t * tile
        cols = []
        for c in range(n_cols):
            part = jnp.zeros((tile, LANES), F32)
            for k in range(CONV_WIDTH):
                start = base + k + front - half
                part = part + pad_ref[c, start:start + tile, :] * w_ref[k:k + 1, c * LANES:(c + 1) * LANES]
            cols.append(part)
        acc = jnp.concatenate(cols, axis=1) + b_ref[...]
        mu = jnp.mean(acc, axis=-1, keepdims=True)
        xc = acc - mu
        var = jnp.mean(xc * xc, axis=-1, keepdims=True)
        y = xc * lax.rsqrt(var + 1e-5) * lnw_ref[...] + lnb_ref[...]
        o_ref[base:base + tile, :] = _silu(y).astype(o_ref.dtype)


def _convmod(uconv, dw_w, dw_b, ln_w, ln_b, layer):
    b, s, c2 = uconv.shape
    ch = c2 // 2
    tile = 64
    vec = lambda: pl.BlockSpec((None, 1, ch), lambda i: (layer, 0, 0))
    return pl.pallas_call(
        functools.partial(_convmod_kernel, seq=s, ch=ch, tile=tile),
        grid=(b,),
        in_specs=[
            pl.BlockSpec((None, s, c2), lambda i: (i, 0, 0)),
            pl.BlockSpec((None, CONV_WIDTH, ch), lambda i: (layer, 0, 0)),
            vec(), vec(), vec(),
        ],
        out_specs=pl.BlockSpec((None, s, ch), lambda i: (i, 0, 0)),
        out_shape=jax.ShapeDtypeStruct((b, s, ch), BF16),
        scratch_shapes=[pltpu.VMEM((ch // LANES, s + 40, LANES), F32)],
        compiler_params=pltpu.CompilerParams(dimension_semantics=("arbitrary",), vmem_limit_bytes=VMEM_LIMIT),
        name="convmod",
    )(uconv, dw_w, dw_b, ln_w, ln_b)


def _pos_features(pos, slope):
    return slope * (pos & -CHUNK).astype(F32), slope * (pos & (CHUNK - 1)).astype(F32)


def _diffattn_kernel(lam_ref, q_ref, k_ref, v_ref, sw_ref, o_ref, kx_s, vx_s, *, tq, seq, slopes, lambda_init):
    p = pl.program_id(1)
    qi = pl.program_id(2)
    n_kt = seq // tq
    hd, dd = HEAD_DIM, DIFF_DIM
    n_pairs = len(slopes) // 2
    assert all(math.log2(s).is_integer() for s in slopes)

    def head_slope(hh):
        slope = jnp.float32(slopes[2 * (n_pairs - 1) + hh])
        for pp in range(n_pairs - 2, -1, -1):
            slope = jnp.where(p == pp, jnp.float32(slopes[2 * pp + hh]), slope)
        return slope

    def feature_lanes(j, shape):
        lane = lax.broadcasted_iota(jnp.int32, shape, 1)
        data = (lane >= dd * j) & (lane < dd * (j + 1))
        return data, lane - dd * (1 - j)

    @pl.when(qi == 0)
    def _():
        pos = lax.broadcasted_iota(jnp.int32, (seq, hd), 0)
        fl = lax.broadcasted_iota(jnp.int32, (seq, hd), 1) & (dd - 1)
        for hh in range(2):
            f1, f2 = _pos_features(pos, head_slope(hh))
            kh = k_ref[:, hd * hh:hd * (hh + 1)].astype(F32)
            feat = jnp.where(fl < 2, 1.0, jnp.where(fl == 2, f1, jnp.where(fl == 3, f2, 0.0)))
            for j in range(2):
                data, _ = feature_lanes(j, (seq, hd))
                kx_s[0, 2 * hh + j] = jnp.where(data, kh, feat).astype(BF16)
                kx_s[1, 2 * hh + j] = jnp.where(data, kh, -feat).astype(BF16)
            vx_s[hh] = jnp.concatenate([v_ref[:, hd * hh:hd * (hh + 1)], jnp.ones((seq, hd), BF16)], axis=1)

    lp = lam_ref[...]
    lam = (jnp.exp(jnp.sum(lp[0:1] * lp[1:2], axis=-1, keepdims=True))
           - jnp.exp(jnp.sum(lp[2:3] * lp[3:4], axis=-1, keepdims=True)) + lambda_init)
    scale = dd ** -0.5
    qpos = lax.broadcasted_iota(jnp.int32, (tq, hd), 0) + qi * tq
    rr = lax.broadcasted_iota(jnp.int32, (tq, tq), 0)
    cc = lax.broadcasted_iota(jnp.int32, (tq, tq), 1)
    ahead = jnp.maximum(cc - rr, 0).astype(F32)
    tiles = []
    for rel in range(n_kt):
        kt = qi + rel
        kt = jnp.where(kt >= n_kt, kt - n_kt, kt)
        tiles.append((jnp.where(kt > qi, 1, 0), pl.ds(pl.multiple_of(kt * tq, tq), tq)))

    def scores(hh, j):
        slope = head_slope(hh)
        f1, f2 = _pos_features(qpos, slope)
        qh = q_ref[:, hd * hh:hd * (hh + 1)].astype(F32) * scale
        data, fl = feature_lanes(j, (tq, hd))
        feat = jnp.where(fl == 0, -f1, jnp.where(fl == 1, -f2, jnp.where(fl < 4, 1.0, 0.0)))
        qx = jnp.where(data, qh, feat).astype(BF16)
        s_tiles = [_mm_nt(qx, kx_s[side, 2 * hh + j, rows, :]) for side, rows in tiles]
        s_tiles[0] = s_tiles[0] + ahead * (-2.0 * slope)
        return jnp.concatenate(s_tiles, axis=1)

    maps = [(hh, j) for hh in range(2) for j in range(2)]
    s_next = scores(*maps[0])
    outs = []
    for n, (hh, j) in enumerate(maps):
        s = s_next
        if n + 1 < len(maps):
            s_next = scores(*maps[n + 1])
        m = jnp.max(s, axis=-1, keepdims=True)
        e = jnp.exp((s - m).astype(BF16))
        ov = None
        for r, (side, rows) in enumerate(tiles):
            part = _mm(e[:, r * tq:(r + 1) * tq], vx_s[hh, rows, :])
            ov = part if ov is None else ov + part
        outs.append(ov[:, :hd] / ov[:, hd:hd + 1])
        if j == 1:
            o = outs[-2] - lam * outs[-1]
            o = _rms(o, sw_ref[...], 1e-5) * (1.0 - lambda_init)
            o_ref[:, hd * hh:hd * (hh + 1)] = o.astype(o_ref.dtype)


def _diffattn(dqkv, lam_all, subln_w, layer, n_heads, tq):
    b, s, w3 = dqkv.shape
    n_pairs = n_heads // 2
    lambda_init = 0.8 - 0.6 * math.exp(-0.3 * layer)
    slopes = tuple(_alibi_slopes(n_heads))
    return pl.pallas_call(
        functools.partial(_diffattn_kernel, tq=tq, seq=s, slopes=slopes, lambda_init=lambda_init),
        grid=(b, n_pairs, s // tq),
        in_specs=[
            pl.BlockSpec((None, 4, DIFF_DIM), lambda i, p, q: (layer, 0, 0)),
            pl.BlockSpec((None, tq, LANES), lambda i, p, q: (i, q, p)),
            pl.BlockSpec((None, s, LANES), lambda i, p, q: (i, 0, n_pairs + p)),
            pl.BlockSpec((None, s, LANES), lambda i, p, q: (i, 0, 2 * n_pairs + p)),
            pl.BlockSpec((None, 1, HEAD_DIM), lambda i, p, q: (layer, 0, 0)),
        ],
        out_specs=pl.BlockSpec((None, tq, LANES), lambda i, p, q: (i, q, p)),
        out_shape=jax.ShapeDtypeStruct((b, s, n_pairs * LANES), BF16),
        scratch_shapes=[
            pltpu.VMEM((2, 4, s, HEAD_DIM), BF16),
            pltpu.VMEM((2, s, LANES), BF16),
        ],
        compiler_params=pltpu.CompilerParams(
            dimension_semantics=("arbitrary", "arbitrary", "arbitrary"), vmem_limit_bytes=VMEM_LIMIT),
        name="diffattn",
    )(lam_all, dqkv, dqkv, dqkv, subln_w)


def _head_blocks(y, low):
    zero = jnp.zeros_like(y)
    return jnp.concatenate([jnp.where(low, y, zero), jnp.where(low, zero, y)], axis=0)


def _unit_tri_inverses(mats, ii, jj, low):
    eye = (ii == jj).astype(F32)
    diag16 = (ii // 16) == (jj // 16)
    bds = [jnp.where(diag16, -a, 0.0) for a in mats]
    ps = [eye + x for x in bds]
    qs = [_mm(x.astype(BF16), _head_blocks(x.astype(BF16), low)) for x in bds]
    for _ in range(2):
        pqs = [_mm(q.astype(BF16), jnp.concatenate([_head_blocks(p.astype(BF16), low),
                                                    _head_blocks(q.astype(BF16), low)], axis=1))
               for p, q in zip(ps, qs)]
        ps = [p + pq[:, :LANES] for p, pq in zip(ps, pqs)]
        qs = [pq[:, LANES:] for pq in pqs]
    ts = [p + _mm(q.astype(BF16), _head_blocks(p.astype(BF16), low)) for p, q in zip(ps, qs)]
    join16 = ((ii // 32) == (jj // 32)) & ((ii // 16) != (jj // 16))
    tbs = [t.astype(BF16) for t in ts]
    tls = [_mm(tb, _head_blocks(jnp.where(join16, a, 0.0).astype(BF16), low)).astype(BF16) for tb, a in zip(tbs, mats)]
    ts = [t - _mm(tl, _head_blocks(tb, low)) for t, tl, tb in zip(ts, tls, tbs)]
    join32 = (ii // 32) != (jj // 32)
    return ts, [jnp.where(join32, a, 0.0).astype(BF16) for a in mats]


def _deltanet_kernel(gq_ref, gk_ref, gv_ref, z_ref, gate_ref, cwq_ref, cwk_ref, cwv_ref, gpar_ref, nw_ref,
                     o_ref,
                     pad_s, kbf_s, qbf_s, qgx_s, kdec_s, rhs_s, gc_s, gr_s, be_s, gl_s, cc_s, wp_s, op_s, qp_s, osum_s,
                     *, seq, n_gate_heads, n_pairs, group_chunks):
    p = pl.program_id(1)
    n_chunks = seq // CHUNK
    hd = HEAD_DIM

    il = lax.broadcasted_iota(jnp.int32, (LANES, LANES), 0)
    jl = lax.broadcasted_iota(jnp.int32, (LANES, LANES), 1)
    same_head = ((il // hd) == (jl // hd)).astype(BF16)

    edge = jnp.zeros((8, LANES), F32)
    pad_s[0:8, :] = edge
    pad_s[8 + seq:, :] = edge

    def short_conv(x_ref, w_ref):
        pad_s[8:8 + seq, :] = x_ref[...].astype(F32)
        y = (pad_s[7:7 + seq, :] * w_ref[0:1, :] + pad_s[8:8 + seq, :] * w_ref[1:2, :]
             + pad_s[9:9 + seq, :] * w_ref[2:3, :])
        return _silu(y)

    def l2n(x):
        xx = x * x
        hi = xx.astype(BF16)
        lo = (xx - hi.astype(F32)).astype(BF16)
        return x * lax.rsqrt(_mm(hi, same_head) + _mm(lo, same_head) + 1e-6)

    q = l2n(short_conv(gq_ref, cwq_ref)) * (hd ** -0.5)
    k = l2n(short_conv(gk_ref, cwk_ref))
    v = short_conv(gv_ref, cwv_ref)
    kbf_s[...] = k.astype(BF16)
    qbf_s[...] = q.astype(BF16)

    nh = n_gate_heads
    raw = gate_ref[...].T[0:GATE_ROWS]
    reps = seq // LANES
    a_log = jnp.tile(gpar_ref[0], (1, reps))
    dt_bias = jnp.tile(gpar_ref[1], (1, reps))
    beta = jax.nn.sigmoid(raw)
    g = -jnp.exp(a_log) * _softplus(raw + dt_bias)
    gate_row = lax.broadcasted_iota(jnp.int32, (GATE_ROWS, seq), 0)
    pos_in_chunk = lax.broadcasted_iota(jnp.int32, (GATE_ROWS, seq), 1) & (CHUNK - 1)
    fwd, rev = g, g
    for sh in (1, 2, 4, 8, 16, 32):
        fwd = fwd + jnp.where(pos_in_chunk >= sh, pltpu.roll(fwd, sh, 1), 0.0)
        rev = rev + jnp.where(pos_in_chunk < CHUNK - sh, pltpu.roll(rev, seq - sh, 1), 0.0)
    gc_r = jnp.where(gate_row < 3 * nh, fwd, rev)

    ir = lax.broadcasted_iota(jnp.int32, (GATE_ROWS, 2 * LANES), 0)
    jr = lax.broadcasted_iota(jnp.int32, (GATE_ROWS, 2 * LANES), 1)

    def expand(x, base, pieces):
        sel = (ir == base + nh * (jr // LANES) + 2 * p + (jr % LANES) // hd).astype(BF16)
        out = None
        for _ in range(pieces):
            xb = x.astype(BF16)
            part = _mm_tn(xb, sel)
            out = part if out is None else out + part
            x = x - xb.astype(F32)
        return out

    be_all = expand(beta, 0, 1)
    gc_all = expand(gc_r, 2 * nh, 3)

    for d in range(2):
        dl = slice(LANES * d, LANES * (d + 1))
        be = be_all[:, dl]
        gc = gc_all[:, dl]
        gc3 = gc.reshape(n_chunks, CHUNK, LANES)
        last = CHUNK - 1 if d == 0 else 0
        total3 = jnp.broadcast_to(gc3[:, last:last + 1, :], gc3.shape)
        eg = jnp.exp(gc)
        kb = k * be
        vb = (v * be).astype(BF16)
        kbe = (kb * eg).astype(BF16)
        gc_s[d] = gc
        be_s[d] = be
        gct = gc.T
        row_a = gct[0:8]
        row_b = gct[hd:hd + 8]
        row_a_r = pltpu.roll(row_a, seq - hd, 1)
        row_b_r = pltpu.roll(row_b, hd, 1)
        low8 = lax.broadcasted_iota(jnp.int32, (8, LANES), 1) < hd
        for c in range(n_chunks):
            col = slice(LANES * (c // 2), LANES * (c // 2 + 1))
            if c % 2 == 0:
                gr_s[d, c] = jnp.where(low8, row_a[:, col], row_b_r[:, col])
            else:
                gr_s[d, c] = jnp.where(low8, row_a_r[:, col], row_b[:, col])
        gl_s[2 * p + d] = jnp.exp(gc3[:, last:last + 1, :] + jnp.zeros((n_chunks, 8, LANES), F32))
        qgx_s[d] = pltpu.roll(q * eg, hd, 1).astype(BF16)
        kdec_s[d] = (k * jnp.exp(total3 - gc3).reshape(seq, LANES)).astype(BF16)
        rhs_s[d, :, 0 * hd:1 * hd] = vb[:, :hd]
        rhs_s[d, :, 1 * hd:2 * hd] = kbe[:, :hd]
        rhs_s[d, :, 2 * hd:3 * hd] = kbe[:, hd:]
        rhs_s[d, :, 3 * hd:4 * hd] = vb[:, hd:]

    ii = lax.broadcasted_iota(jnp.int32, (CHUNK, LANES), 0)
    jj = lax.broadcasted_iota(jnp.int32, (CHUNK, LANES), 1) & (CHUNK - 1)

    low = lax.broadcasted_iota(jnp.int32, (CHUNK, LANES), 1) < hd
    zeros_rhs = jnp.zeros((CHUNK, LANES), BF16)

    def two_blocks(x):
        return jnp.concatenate([jnp.concatenate([x[:, :LANES], zeros_rhs], axis=1),
                                jnp.concatenate([zeros_rhs, x[:, LANES:]], axis=1)], axis=0)

    def prep(g, carry):
        keys, a_list, aqk_list = [], [], []
        for ci in range(group_chunks):
            c = g * group_chunks + ci
            rows = pl.ds(pl.multiple_of(c * CHUNK, CHUNK), CHUNK)
            kpk = kbf_s[rows, :]
            kq = _mm_nt(jnp.concatenate([kpk, qbf_s[rows, :]], axis=0), _head_blocks(kpk, low))
            kk, qk = kq[:CHUNK], kq[CHUNK:]
            for d in range(2):
                diff = gc_s[d, rows, :] - jnp.concatenate([gr_s[d, c]] * (CHUNK // 8), axis=0)
                incl = (ii >= jj) if d == 0 else (ii <= jj)
                strict = (ii > jj) if d == 0 else (ii < jj)
                dec = jnp.exp(jnp.where(incl, diff, -1e30))
                a_list.append(jnp.where(strict, kk * dec, 0.0) * be_s[d, rows, :])
                aqk_list.append((qk * dec).astype(BF16))
                keys.append((c, rows, d))
        ts, joins = _unit_tri_inverses(a_list, ii, jj, low)
        tbs = [t.astype(BF16) for t in ts]
        xs = [_mm(tb, two_blocks(rhs_s[d, rows, :])) for tb, (c, rows, d) in zip(tbs, keys)]
        tls = [_mm(tb, _head_blocks(lj, low)).astype(BF16) for tb, lj in zip(tbs, joins)]
        uws = [(x - _mm(tl, two_blocks(x.astype(BF16)))).astype(BF16)
               for x, tl in zip(xs, tls)]
        for uw, aqk, (c, rows, d) in zip(uws, aqk_list, keys):
            kd = _mm_tn(kdec_s[d, rows, :], uw)
            kd_a, kd_b = kd[:hd, :LANES], kd[hd:, LANES:]
            ao = _mm(aqk, two_blocks(uw))
            ao_a, ao_b = ao[:, :LANES], ao[:, LANES:]
            idx = 2 * p + d
            cc_s[idx, c] = jnp.where(low, kd_a, kd_b).astype(BF16)
            wp_s[idx, c] = jnp.where(low, kd_b, kd_a).astype(BF16)
            op_s[idx, c] = jnp.where(low, ao_a, ao_b).astype(BF16)
            qp_s[idx, c] = (qgx_s[d, rows, :].astype(F32) - jnp.where(low, ao_b, ao_a)).astype(BF16)
        return carry

    lax.fori_loop(0, n_chunks // group_chunks, prep, 0)

    @pl.when(p == n_pairs - 1)
    def _():
        def scan(n, states):
            new = []
            for pp in range(n_pairs):
                for d in range(2):
                    idx = 2 * pp + d
                    c = n if d == 0 else n_chunks - 1 - n
                    rows = pl.ds(pl.multiple_of(c * CHUNK, CHUNK), CHUNK)
                    st = states[idx]
                    anti = jnp.concatenate([jnp.where(low, 0.0, st), jnp.where(low, st, 0.0)], axis=0).astype(BF16)
                    osum_s[idx, rows, :] = _mm(qp_s[idx, c], anti) + op_s[idx, c].astype(F32)
                    gl = jnp.concatenate([gl_s[idx, c]] * (hd // 8), axis=0)
                    new.append(st * gl - _mm(wp_s[idx, c], anti) + cc_s[idx, c].astype(F32))
            return tuple(new)

        zero = jnp.zeros((hd, LANES), F32)
        lax.fori_loop(0, n_chunks, scan, (zero,) * (2 * n_pairs))

        for pp in range(n_pairs):
            o = osum_s[2 * pp] + osum_s[2 * pp + 1]
            ms = _mm_exact_rhs(o * o, same_head) * (1.0 / hd)
            zz = z_ref[:, LANES * pp:LANES * (pp + 1)].astype(F32)
            y = o * lax.rsqrt(ms + NORM_EPS) * nw_ref[...] * _silu(zz)
            o_ref[:, LANES * pp:LANES * (pp + 1)] = y.astype(o_ref.dtype)


def _deltanet(gqkv, gz, gates, conv_w, gpar, norm_w2, layer, n_heads):
    b, s, w3 = gqkv.shape
    n_pairs = n_heads // 2
    n_chunks = s // CHUNK
    n_dir = 2 * n_pairs
    blk = lambda off: pl.BlockSpec((None, s, LANES), lambda i, p: (i, 0, off + p))
    cw = lambda off: pl.BlockSpec((None, SHORT_CONV, LANES), lambda i, p: (layer, 0, off + p))
    return pl.pallas_call(
        functools.partial(_deltanet_kernel, seq=s, n_gate_heads=n_heads, n_pairs=n_pairs,
                          group_chunks=min(16, n_chunks)),
        grid=(b, n_pairs),
        in_specs=[
            blk(0), blk(n_pairs), blk(2 * n_pairs),
            pl.BlockSpec((None, s, n_pairs * LANES), lambda i, p: (i, 0, 0)),
            pl.BlockSpec((None, s, LANES), lambda i, p: (i, 0, 0)),
            cw(0), cw(n_pairs), cw(2 * n_pairs),
            pl.BlockSpec((None, 2, GATE_ROWS, LANES), lambda i, p: (layer, 0, 0, 0)),
            pl.BlockSpec((None, 1, LANES), lambda i, p: (layer, 0, 0)),
        ],
        out_specs=pl.BlockSpec((None, s, n_pairs * LANES), lambda i, p: (i, 0, 0)),
        out_shape=jax.ShapeDtypeStruct((b, s, n_pairs * LANES), BF16),
        scratch_shapes=[
            pltpu.VMEM((s + 16, LANES), F32),
            pltpu.VMEM((s, LANES), BF16),
            pltpu.VMEM((s, LANES), BF16),
            pltpu.VMEM((2, s, LANES), BF16),
            pltpu.VMEM((2, s, LANES), BF16),
            pltpu.VMEM((2, s, 2 * LANES), BF16),
            pltpu.VMEM((2, s, LANES), F32),
            pltpu.VMEM((2, n_chunks, 8, LANES), F32),
            pltpu.VMEM((2, s, LANES), F32),
            pltpu.VMEM((n_dir, n_chunks, 8, LANES), F32),
            pltpu.VMEM((n_dir, n_chunks, HEAD_DIM, LANES), BF16),
            pltpu.VMEM((n_dir, n_chunks, HEAD_DIM, LANES), BF16),
            pltpu.VMEM((n_dir, n_chunks, CHUNK, LANES), BF16),
            pltpu.VMEM((n_dir, n_chunks, CHUNK, LANES), BF16),
            pltpu.VMEM((n_dir, s, LANES), F32),
        ],
        compiler_params=pltpu.CompilerParams(
            dimension_semantics=("arbitrary", "arbitrary"), vmem_limit_bytes=VMEM_LIMIT),
        name="deltanet",
    )(gqkv, gqkv, gqkv, gz, gates, conv_w, conv_w, conv_w, gpar, norm_w2)


def _mix_ffn_kernel(yc_ref, yd_ref, yg_ref, wo_ref, x_ref, nwm_ref, nw1_ref, w1_ref, w2_ref, nw2_ref, o_ref,
                    xm_s, hb_s, acc_s):
    j = pl.program_id(1)

    @pl.when(j == 0)
    def _():
        y = _mm(jnp.concatenate([yc_ref[...], yd_ref[...], yg_ref[...]], axis=1), wo_ref[...])
        xm = x_ref[...] + _rms(y, nwm_ref[...], NORM_EPS)
        xm_s[...] = xm
        hb_s[...] = _rms(xm, nw1_ref[...], NORM_EPS).astype(BF16)
        acc_s[...] = jnp.zeros_like(acc_s)

    a = _mm(hb_s[...], w1_ref[...])
    a = jnp.square(jnp.maximum(a, 0.0)).astype(BF16)
    acc_s[...] += _mm(a, w2_ref[...])

    @pl.when(j == pl.num_programs(1) - 1)
    def _():
        o_ref[...] = xm_s[...] + _rms(acc_s[...], nw2_ref[...], NORM_EPS)


def _mix_ffn(yc, yd, yg, wo_all, x2, nwm, nw1, w1_all, w2_all, nw2, layer, tm, tf):
    m, d = x2.shape
    dff = w1_all.shape[-1]
    row = lambda a: pl.BlockSpec((tm, a.shape[-1]), lambda i, j: (i, 0))
    vec = lambda: pl.BlockSpec((None, 1, d), lambda i, j: (layer, 0, 0))
    return pl.pallas_call(
        _mix_ffn_kernel,
        grid=(m // tm, dff // tf),
        in_specs=[
            row(yc), row(yd), row(yg),
            pl.BlockSpec((None, d, d), lambda i, j: (layer, 0, 0)),
            pl.BlockSpec((tm, d), lambda i, j: (i, 0)),
            vec(), vec(),
            pl.BlockSpec((None, d, tf), lambda i, j: (layer, 0, j)),
            pl.BlockSpec((None, tf, d), lambda i, j: (layer, j, 0)),
            vec(),
        ],
        out_specs=pl.BlockSpec((tm, d), lambda i, j: (i, 0)),
        out_shape=jax.ShapeDtypeStruct((m, d), F32),
        scratch_shapes=[pltpu.VMEM((tm, d), F32), pltpu.VMEM((tm, d), BF16), pltpu.VMEM((tm, d), F32)],
        compiler_params=pltpu.CompilerParams(
            dimension_semantics=("arbitrary", "arbitrary"), vmem_limit_bytes=VMEM_LIMIT),
        name="mix_ffn",
    )(yc, yd, yg, wo_all, x2, nwm, nw1, w1_all, w2_all, nw2)


def kernel(x, w_in, w_out, pre_mix_w, post_mix_w, pre_mlp_w, post_mlp_w, w_ff1, w_ff2, conv_dw_w, conv_dw_b, conv_ln_w, conv_ln_b, diff_lambda_q1, diff_lambda_k1, diff_lambda_q2, diff_lambda_k2, diff_subln_w, delta_conv_w, delta_A_log, delta_dt_bias, delta_norm_w):
    b, s, d = x.shape
    depth = w_in.shape[0]
    conv_ch = conv_dw_w.shape[-1]
    delta_w = delta_conv_w.shape[-1] // 3
    n_delta_heads = delta_w // HEAD_DIM
    in_w = w_in.shape[-1]
    diff_w = (in_w - 2 * conv_ch - 4 * delta_w - 4 * n_delta_heads) // 3
    n_diff_heads = diff_w // HEAD_DIM
    n_gates = 4 * n_delta_heads
    splits = (2 * conv_ch, 3 * diff_w, 3 * delta_w, delta_w, LANES)
    m = b * s
    tm = min(1024, m)

    w_in_b = jnp.pad(w_in, ((0, 0), (0, 0), (0, LANES - n_gates))).astype(BF16)
    w_out_b = w_out.astype(BF16)
    w1_b = w_ff1.astype(BF16)
    w2_b = w_ff2.astype(BF16)
    row3 = lambda a: a.reshape(depth, 1, a.shape[-1])
    lam_all = jnp.stack([diff_lambda_q1, diff_lambda_k1, diff_lambda_q2, diff_lambda_k2], axis=1)
    gate_par = jnp.stack([delta_A_log.reshape(depth, -1), delta_dt_bias.reshape(depth, -1)], axis=1)
    gate_par = jnp.pad(gate_par, ((0, 0), (0, 0), (2 * n_delta_heads, GATE_ROWS - n_gates)))
    gate_par = jnp.broadcast_to(gate_par[..., None], (depth, 2, GATE_ROWS, LANES))
    delta_nw2 = row3(jnp.concatenate([delta_norm_w, delta_norm_w], axis=-1))

    x2 = x.reshape(m, d)
    for l in range(depth):
        uconv, dqkv, gqkv, gz, gates = _in_proj(x2, row3(pre_mix_w), w_in_b, l, splits, tm)
        y_conv = _convmod(uconv.reshape(b, s, -1), conv_dw_w, row3(conv_dw_b), row3(conv_ln_w),
                          row3(conv_ln_b), l)
        y_diff = _diffattn(dqkv.reshape(b, s, -1), lam_all, row3(diff_subln_w), l, n_diff_heads, min(256, s))
        y_delta = _deltanet(gqkv.reshape(b, s, -1), gz.reshape(b, s, -1), gates.reshape(b, s, -1),
                            delta_conv_w, gate_par, delta_nw2, l, n_delta_heads)
        x2 = _mix_ffn(y_conv.reshape(m, -1), y_diff.reshape(m, -1), y_delta.reshape(m, -1), w_out_b, x2,
                      row3(post_mix_w), row3(pre_mlp_w), w1_b, w2_b, row3(post_mlp_w), l, min(1024, m), 1024)
    return x2.reshape(b, s, d)
```
